```python
import math
import jax, jax.numpy as jnp
from jax import lax
import numpy as np

D_MODEL = 4096
BATCH = 4
SEQ = 4096
DEPTH = 4
DEC_BATCH = 32
DEC_SEQ = 32
PAST_LEN = 1024

CHUNK = 64
Q_BLOCK = 128
N_HEADS = 8
HEAD_DIM = 128
QK_WIDTH = N_HEADS * 2 * HEAD_DIM
V_WIDTH = N_HEADS * 2 * HEAD_DIM
SSM_WIDTH = D_MODEL // 2
SSM_GROUP = 16
SSM_GROUPS = SSM_WIDTH // SSM_GROUP
SSM_STATE = 64
DT_MIN = 1e-3
DT_MAX = 1e-1
IN_COLS = 2 * QK_WIDTH + V_WIDTH + SSM_WIDTH + 2 * D_MODEL
SPLITS = [QK_WIDTH, 2 * QK_WIDTH, 2 * QK_WIDTH + V_WIDTH, 2 * QK_WIDTH + V_WIDTH + SSM_WIDTH]
N_BUCKETS = 32
REL_MAX_DIST = 128
D_FF = 11008
N_EXPERTS = 8
TOP_K = 2
D_FF_EXPERT = 2816
N_DENSE = (DEPTH + 1) // 2
N_MOE = DEPTH // 2
EPS = 1e-6

kernel_name = 'streaming_diffattn_s5_hybrid_step'


def _rmsnorm(x, g):
    xf = x.astype(jnp.float32)
    y = xf * lax.rsqrt(jnp.mean(xf * xf, axis=-1, keepdims=True) + EPS)
    return (y * g.astype(jnp.float32)).astype(x.dtype)


def _rel_bucket(rel):
    half = N_BUCKETS // 2
    max_exact = half // 2
    n = jnp.abs(rel)
    nf = jnp.maximum(n, 1).astype(jnp.float32)
    large = max_exact + (jnp.log(nf / max_exact) / math.log(REL_MAX_DIST / max_exact)
                         * (half - max_exact)).astype(jnp.int32)
    large = jnp.minimum(large, half - 1)
    return jnp.where(rel > 0, half, 0) + jnp.where(n < max_exact, n, large)


def _diff_attn_block(q, q_pos, k, v, k_pos, lam, rel_bias):
    s = jnp.einsum('bqhcd,bkhcd->bhcqk', q, k).astype(jnp.float32) * (HEAD_DIM ** -0.5)
    rel = k_pos[None, :] - q_pos[:, None]
    bias = jnp.transpose(rel_bias.astype(jnp.float32)[_rel_bucket(rel)], (2, 0, 1))
    visible = (k_pos[None, :] // CHUNK) <= (q_pos[:, None] // CHUNK)
    s = jnp.where(visible, s + bias[None, :, None], -jnp.inf)
    p = jax.nn.softmax(s, axis=-1)
    w = p[:, :, 0] - lam * p[:, :, 1]
    return jnp.einsum('bhqk,bkhe->bqhe', w.astype(v.dtype), v)


def _diff_attention(q, q_pos, k, v, k_pos, lam, rel_bias):
    bsz, lq = q.shape[0], q.shape[1]
    if lq <= Q_BLOCK:
        return _diff_attn_block(q, q_pos, k, v, k_pos, lam, rel_bias)
    nb = lq // Q_BLOCK
    qb = jnp.moveaxis(q.reshape(bsz, nb, Q_BLOCK, N_HEADS, 2, HEAD_DIM), 1, 0)
    pb = q_pos.reshape(nb, Q_BLOCK)
    out = lax.map(lambda qp: _diff_attn_block(qp[0], qp[1], k, v, k_pos, lam, rel_bias), (qb, pb))
    return jnp.moveaxis(out, 0, 1).reshape(bsz, lq, N_HEADS, 2 * HEAD_DIM)


def _ssm_combine(e1, e2):
    a1r, a1i, b1r, b1i = e1
    a2r, a2i, b2r, b2i = e2
    return (a2r * a1r - a2i * a1i, a2r * a1i + a2i * a1r,
            a2r * b1r - a2i * b1i + b2r, a2r * b1i + a2i * b1r + b2i)


def _s5(u, h0_re, h0_im, a_re, a_im, log_dt, b_re, b_im, c_re, c_im, d):
    f32 = jnp.float32
    bsz, L = u.shape[0], u.shape[1]
    dt = jnp.exp(log_dt.astype(f32))[:, None]
    lr = a_re.astype(f32)
    li = a_im.astype(f32)
    mag = jnp.exp(lr * dt)
    ab_re = mag * jnp.cos(li * dt)
    ab_im = mag * jnp.sin(li * dt)
    den = lr * lr + li * li
    num_re = ab_re - 1.0
    coef_re = (num_re * lr + ab_im * li) / den
    coef_im = (ab_im * lr - num_re * li) / den
    br = b_re.astype(f32)
    bi = b_im.astype(f32)
    bb_re = coef_re[..., None] * br - coef_im[..., None] * bi
    bb_im = coef_re[..., None] * bi + coef_im[..., None] * br
    uf = u.astype(f32)
    ug = uf.reshape(bsz, L, SSM_GROUPS, SSM_GROUP)
    bu_re = jnp.einsum('blgc,gnc->blgn', ug, bb_re)
    bu_im = jnp.einsum('blgc,gnc->blgn', ug, bb_im)
    hr0 = h0_re.astype(f32)
    hi0 = h0_im.astype(f32)
    bu_re = bu_re.at[:, 0].add(ab_re * hr0 - ab_im * hi0)
    bu_im = bu_im.at[:, 0].add(ab_re * hi0 + ab_im * hr0)
    a_r = jnp.broadcast_to(ab_re, bu_re.shape)
    a_i = jnp.broadcast_to(ab_im, bu_im.shape)
    _, _, h_re, h_im = lax.associative_scan(_ssm_combine, (a_r, a_i, bu_re, bu_im), axis=1)
    y = (jnp.einsum('blgn,gcn->blgc', h_re, c_re.astype(f32))
         - jnp.einsum('blgn,gcn->blgc', h_im, c_im.astype(f32)))
    y = y.reshape(bsz, L, SSM_WIDTH) + d.astype(f32) * uf
    return y.astype(u.dtype), h_re[:, -1].astype(h0_re.dtype), h_im[:, -1].astype(h0_im.dtype)


def _swiglu(x, wg, wu, wd):
    h = jax.nn.silu(jnp.einsum('bld,df->blf', x, wg)) * jnp.einsum('bld,df->blf', x, wu)
    return jnp.einsum('blf,fd->bld', h, wd)


def _moe_swiglu(x, w_router, b_router, wg, wu, wd):
    logits = jnp.einsum('bld,de->ble', x, w_router).astype(jnp.float32) + b_router.astype(jnp.float32)
    top_v, top_i = lax.top_k(logits, TOP_K)
    top_w = jax.nn.softmax(top_v, axis=-1)
    gates = jnp.sum(jax.nn.one_hot(top_i, N_EXPERTS, dtype=jnp.float32) * top_w[..., None], axis=-2)
    out = jnp.zeros(x.shape, jnp.float32)
    for e in range(N_EXPERTS):
        out = out + gates[..., e:e + 1] * _swiglu(x, wg[e], wu[e], wd[e]).astype(jnp.float32)
    return out.astype(x.dtype)


def _trunk(x, past_k, past_v, h0_re, h0_im, P):
    bsz, L = x.shape[0], x.shape[1]
    past_len = 0 if past_k is None else past_k.shape[2]
    q_pos = past_len + jnp.arange(L, dtype=jnp.int32)
    k_pos = jnp.arange(past_len + L, dtype=jnp.int32)
    new_k, new_v, new_hr, new_hi = [], [], [], []
    for l in range(DEPTH):
        xn = _rmsnorm(x, P['norm_mix_g'][l])
        proj = jnp.einsum('bld,dc->blc', xn, P['w_in'][l])
        q, k, v, u, g = jnp.split(proj, SPLITS, axis=-1)
        q = _rmsnorm(q.reshape(bsz, L, N_HEADS, 2, HEAD_DIM), P['q_norm_g'][l])
        k = _rmsnorm(k.reshape(bsz, L, N_HEADS, 2, HEAD_DIM), P['k_norm_g'][l])
        v = v.reshape(bsz, L, N_HEADS, 2 * HEAD_DIM)
        if past_k is None:
            k_all, v_all = k, v
        else:
            k_all = jnp.concatenate([past_k[l], k], axis=1)
            v_all = jnp.concatenate([past_v[l], v], axis=1)
        lam_init = 0.8 - 0.6 * math.exp(-0.3 * l)
        f32 = jnp.float32
        lam = (jnp.exp(jnp.sum(P['lambda_q1'][l].astype(f32) * P['lambda_k1'][l].astype(f32)))
               - jnp.exp(jnp.sum(P['lambda_q2'][l].astype(f32) * P['lambda_k2'][l].astype(f32))) + lam_init)
        o = _diff_attention(q, q_pos, k_all, v_all, k_pos, lam, P['rel_bias'])
        o = _rmsnorm(o, P['subln_g'][l]) * (1.0 - lam_init)
        attn_b = jnp.einsum('blc,cd->bld', o.reshape(bsz, L, V_WIDTH), P['w_attn_proj'][l])
        hr0 = jnp.zeros((bsz, SSM_GROUPS, SSM_STATE), x.dtype) if h0_re is None else h0_re[l]
        hi0 = jnp.zeros((bsz, SSM_GROUPS, SSM_STATE), x.dtype) if h0_im is None else h0_im[l]
        y, hr, hi = _s5(u, hr0, hi0, P['ssm_a_re'][l], P['ssm_a_im'][l], P['ssm_log_dt'][l],
                        P['ssm_b_re'][l], P['ssm_b_im'][l], P['ssm_c_re'][l], P['ssm_c_im'][l], P['ssm_d'][l])
        hs = jax.nn.gelu(y)
        hs = hs * jax.nn.sigmoid(jnp.einsum('blc,ce->ble', hs, P['w_glu'][l]) + P['b_glu'][l])
        ssm_b = jnp.einsum('blc,cd->bld', hs, P['w_ssm_proj'][l])
        gates = jax.nn.sigmoid(g + P['b_gate'][l])
        ga, gs = jnp.split(gates, 2, axis=-1)
        x = x + jnp.einsum('bld,de->ble', ga * attn_b + gs * ssm_b, P['w_out'][l])
        xn = _rmsnorm(x, P['norm_ffn_g'][l])
        if l % 2 == 0:
            i = l // 2
            x = x + _swiglu(xn, P['w_ff_gate'][i], P['w_ff_up'][i], P['w_ff_down'][i])
        else:
            i = l // 2
            x = x + _moe_swiglu(xn, P['w_router'][i], P['b_router'][i],
                                P['w_e_gate'][i], P['w_e_up'][i], P['w_e_down'][i])
        new_k.append(k)
        new_v.append(v)
        new_hr.append(hr)
        new_hi.append(hi)
    return x, jnp.stack(new_k), jnp.stack(new_v), jnp.stack(new_hr), jnp.stack(new_hi)


def setup_inputs(seed: int = 0) -> dict:
    key = jax.random.key(seed)
    ks = jax.random.split(key, 40)
    f32 = jnp.float32
    nrm = lambda k, s: jax.random.normal(k, s, f32)
    w = lambda k, s, fan_in: nrm(k, s) * fan_in ** -0.5
    gain = lambda k, s: 1.0 + 0.01 * nrm(k, s)
    a_im0 = math.pi * jnp.arange(SSM_STATE, dtype=f32)
    return {
        'x_prompt': nrm(ks[0], (BATCH, SEQ, D_MODEL)),
        'x_sample': nrm(ks[1], (DEC_BATCH, DEC_SEQ, D_MODEL)),
        'cache_k': nrm(ks[2], (DEPTH, DEC_BATCH, PAST_LEN, N_HEADS, 2, HEAD_DIM)),
        'cache_v': nrm(ks[3], (DEPTH, DEC_BATCH, PAST_LEN, N_HEADS, 2 * HEAD_DIM)),
        'state_ssm_re': 0.5 * nrm(ks[4], (DEPTH, DEC_BATCH, SSM_GROUPS, SSM_STATE)),
        'state_ssm_im': 0.5 * nrm(ks[5], (DEPTH, DEC_BATCH, SSM_GROUPS, SSM_STATE)),
        'rel_bias': 0.1 * nrm(ks[6], (N_BUCKETS, N_HEADS)),
        'norm_mix_g': gain(ks[7], (DEPTH, D_MODEL)),
        'w_in': w(ks[8], (DEPTH, D_MODEL, IN_COLS), D_MODEL),
        'b_gate': 0.02 * nrm(ks[9], (DEPTH, 2 * D_MODEL)),
        'q_norm_g': gain(ks[10], (DEPTH, HEAD_DIM)),
        'k_norm_g': gain(ks[11], (DEPTH, HEAD_DIM)),
        'lambda_q1': 0.1 * nrm(ks[12], (DEPTH, HEAD_DIM)),
        'lambda_k1': 0.1 * nrm(ks[13], (DEPTH, HEAD_DIM)),
        'lambda_q2': 0.1 * nrm(ks[14], (DEPTH, HEAD_DIM)),
        'lambda_k2': 0.1 * nrm(ks[15], (DEPTH, HEAD_DIM)),
        'subln_g': gain(ks[16], (DEPTH, 2 * HEAD_DIM)),
        'w_attn_proj': w(ks[17], (DEPTH, V_WIDTH, D_MODEL), V_WIDTH),
        'ssm_a_re': -0.5 + 0.01 * nrm(ks[18], (DEPTH, SSM_GROUPS, SSM_STATE)),
        'ssm_a_im': a_im0 + 0.01 * nrm(ks[19], (DEPTH, SSM_GROUPS, SSM_STATE)),
        'ssm_log_dt': jax.random.uniform(ks[20], (DEPTH, SSM_GROUPS), f32, math.log(DT_MIN), math.log(DT_MAX)),
        'ssm_b_re': w(ks[21], (DEPTH, SSM_GROUPS, SSM_STATE, SSM_GROUP), 2 * SSM_GROUP),
        'ssm_b_im': w(ks[22], (DEPTH, SSM_GROUPS, SSM_STATE, SSM_GROUP), 2 * SSM_GROUP),
        'ssm_c_re': w(ks[23], (DEPTH, SSM_GROUPS, SSM_GROUP, SSM_STATE), 2 * SSM_STATE),
        'ssm_c_im': w(ks[24], (DEPTH, SSM_GROUPS, SSM_GROUP, SSM_STATE), 2 * SSM_STATE),
        'ssm_d': nrm(ks[25], (DEPTH, SSM_WIDTH)),
        'w_glu': w(ks[26], (DEPTH, SSM_WIDTH, SSM_WIDTH), SSM_WIDTH),
        'b_glu': 0.02 * nrm(ks[27], (DEPTH, SSM_WIDTH)),
        'w_ssm_proj': w(ks[28], (DEPTH, SSM_WIDTH, D_MODEL), SSM_WIDTH),
        'w_out': w(ks[29], (DEPTH, D_MODEL, D_MODEL), D_MODEL),
        'norm_ffn_g': gain(ks[30], (DEPTH, D_MODEL)),
        'w_ff_gate': w(ks[31], (N_DENSE, D_MODEL, D_FF), D_MODEL),
        'w_ff_up': w(ks[32], (N_DENSE, D_MODEL, D_FF), D_MODEL),
        'w_ff_down': w(ks[33], (N_DENSE, D_FF, D_MODEL), D_FF),
        'w_router': w(ks[34], (N_MOE, D_MODEL, N_EXPERTS), D_MODEL),
        'b_router': 0.01 * nrm(ks[35], (N_MOE, N_EXPERTS)),
        'w_e_gate': w(ks[36], (N_MOE, N_EXPERTS, D_MODEL, D_FF_EXPERT), D_MODEL),
        'w_e_up': w(ks[37], (N_MOE, N_EXPERTS, D_MODEL, D_FF_EXPERT), D_MODEL),
        'w_e_down': w(ks[38], (N_MOE, N_EXPERTS, D_FF_EXPERT, D_MODEL), D_FF_EXPERT),
    }


def reference(x_prompt, x_sample, cache_k, cache_v, state_ssm_re, state_ssm_im, rel_bias,
              norm_mix_g, w_in, b_gate, q_norm_g, k_norm_g, lambda_q1, lambda_k1, lambda_q2,
              lambda_k2, subln_g, w_attn_proj, ssm_a_re, ssm_a_im, ssm_log_dt, ssm_b_re, ssm_b_im,
              ssm_c_re, ssm_c_im, ssm_d, w_glu, b_glu, w_ssm_proj, w_out, norm_ffn_g, w_ff_gate,
              w_ff_up, w_ff_down, w_router, b_router, w_e_gate, w_e_up, w_e_down):
    P = dict(rel_bias=rel_bias, norm_mix_g=norm_mix_g, w_in=w_in, b_gate=b_gate,
             q_norm_g=q_norm_g, k_norm_g=k_norm_g, lambda_q1=lambda_q1, lambda_k1=lambda_k1,
             lambda_q2=lambda_q2, lambda_k2=lambda_k2, subln_g=subln_g, w_attn_proj=w_attn_proj,
             ssm_a_re=ssm_a_re, ssm_a_im=ssm_a_im, ssm_log_dt=ssm_log_dt, ssm_b_re=ssm_b_re,
             ssm_b_im=ssm_b_im, ssm_c_re=ssm_c_re, ssm_c_im=ssm_c_im, ssm_d=ssm_d, w_glu=w_glu,
             b_glu=b_glu, w_ssm_proj=w_ssm_proj, w_out=w_out, norm_ffn_g=norm_ffn_g,
             w_ff_gate=w_ff_gate, w_ff_up=w_ff_up, w_ff_down=w_ff_down, w_router=w_router,
             b_router=b_router, w_e_gate=w_e_gate, w_e_up=w_e_up, w_e_down=w_e_down)
    y_prompt, k_p, v_p, hr_p, hi_p = _trunk(x_prompt, None, None, None, None, P)
    y_sample, k_s, v_s, hr_s, hi_s = _trunk(x_sample, cache_k, cache_v, state_ssm_re, state_ssm_im, P)
    return (y_prompt, y_sample, k_p, v_p, hr_p, hi_p, k_s, v_s, hr_s, hi_s)
```

```python
import functools
import math

import jax
import jax.numpy as jnp
from jax import lax
from jax.experimental import pallas as pl
from jax.experimental.pallas import tpu as pltpu

F32 = jnp.float32
BF16 = jnp.bfloat16

CHUNK = 64
REL_MAX_DIST = 128
EPS = 1e-6
NEG = -1e30
LANES = 128
VMEM_LIMIT = 56 * 1024 * 1024
S5_T = 16


def _pick(n, cands):
    for c in cands:
        if n % c == 0:
            return c
    return n


def _cparams(sem):
    return pltpu.CompilerParams(dimension_semantics=sem, vmem_limit_bytes=VMEM_LIMIT)


def _rmsnorm_kernel(x_ref, g_ref, o_ref):
    x = x_ref[...]
    y = x * lax.rsqrt(jnp.mean(x * x, axis=-1, keepdims=True) + EPS)
    o_ref[...] = (y * g_ref[...]).astype(o_ref.dtype)


def _rmsnorm(x, g, out_dtype):
    m, d = x.shape
    tm = _pick(m, (512, 256, 128, 64, 32, 16, 8))
    return pl.pallas_call(
        _rmsnorm_kernel,
        grid=(m // tm,),
        in_specs=[pl.BlockSpec((tm, d), lambda i: (i, 0)), pl.BlockSpec((1, d), lambda i: (0, 0))],
        out_specs=pl.BlockSpec((tm, d), lambda i: (i, 0)),
        out_shape=jax.ShapeDtypeStruct((m, d), out_dtype),
        compiler_params=_cparams(("parallel",)),
    )(x, g.reshape(1, d).astype(F32))


def _mm_kernel(*refs, nb, nx, nk, epi, grouped):
    if grouped:
        nact_ref = refs[1]
        refs = refs[2:]
    a_ref = refs[0]
    b_refs = refs[1:1 + nb]
    x_refs = refs[1 + nb:1 + nb + nx]
    o_ref = refs[1 + nb + nx]
    acc_refs = refs[2 + nb + nx:]

    def body():
        a = a_ref[...].astype(BF16)
        if nk == 1:
            accs = [jnp.dot(a, b[...], preferred_element_type=F32) for b in b_refs]
            o_ref[...] = epi(*accs, *[x[...] for x in x_refs]).astype(o_ref.dtype)
            return
        k = pl.program_id(2)

        @pl.when(k == 0)
        def _():
            for acc in acc_refs:
                acc[...] = jnp.zeros_like(acc)

        for acc, b in zip(acc_refs, b_refs):
            acc[...] += jnp.dot(a, b[...], preferred_element_type=F32)

        @pl.when(k == nk - 1)
        def _():
            o_ref[...] = epi(*[acc[...] for acc in acc_refs], *[x[...] for x in x_refs]).astype(o_ref.dtype)

    if grouped:
        pl.when(pl.program_id(0) < nact_ref[0])(body)
    else:
        body()


def _matmul(a, bs, epi, out_dtype, extras=(), group=None, tm=None, tn=None, tk=None):
    m, kdim = a.shape
    n = bs[0].shape[-1]
    tm = tm or _pick(m, (1024, 512, 256, 128, 64, 32, 16, 8))
    tn = tn or _pick(n, (1024, 512, 256, 128))
    tk = tk or (kdim if kdim <= 2048 else _pick(kdim, (2816, 2048, 1024, 512, 256, 128)))
    nk = kdim // tk
    grouped = group is not None

    if grouped:
        def row(i, te, na):
            return jnp.minimum(i, na[0] - 1)
        a_spec = pl.BlockSpec((tm, tk), lambda i, j, k, te, na: (row(i, te, na), k))
        b_specs = [pl.BlockSpec((None, tk, tn), lambda i, j, k, te, na: (te[row(i, te, na)], k, j)) for _ in bs]
        o_spec = pl.BlockSpec((tm, tn), lambda i, j, k, te, na: (i, j))
        x_specs = []
        assert not extras
    else:
        a_spec = pl.BlockSpec((tm, tk), lambda i, j, k: (i, k))
        b_specs = [pl.BlockSpec((tk, tn), lambda i, j, k: (k, j)) for _ in bs]
        o_spec = pl.BlockSpec((tm, tn), lambda i, j, k: (i, j))
        x_specs = []
        for arr, kind, off in extras:
            if kind == "tile":
                x_specs.append(pl.BlockSpec((tm, tn), lambda i, j, k, off=off: (i, j + off)))
            else:
                x_specs.append(pl.BlockSpec((1, tn), lambda i, j, k, off=off: (0, j + off)))

    scratch = [pltpu.VMEM((tm, tn), F32) for _ in bs] if nk > 1 else []
    kern = functools.partial(_mm_kernel, nb=len(bs), nx=len(extras), nk=nk, epi=epi, grouped=grouped)
    grid_spec = pltpu.PrefetchScalarGridSpec(
        num_scalar_prefetch=2 if grouped else 0,
        grid=(m // tm, n // tn, nk),
        in_specs=[a_spec, *b_specs, *x_specs],
        out_specs=o_spec,
        scratch_shapes=scratch,
    )
    args = ([group[0], group[1]] if grouped else []) + [a, *bs, *[e[0] for e in extras]]
    return pl.pallas_call(
        kern,
        grid_spec=grid_spec,
        out_shape=jax.ShapeDtypeStruct((m, n), out_dtype),
        compiler_params=_cparams(("arbitrary" if grouped else "parallel", "arbitrary" if grouped else "parallel",
                                  "arbitrary")),
    )(*args)


def _epi_id(acc):
    return acc


def _epi_residual(acc, x):
    return x + acc


def _epi_swiglu(g, u):
    return jax.nn.silu(g) * u


def _epi_glu(acc, hs, b):
    return hs.astype(F32) * jax.nn.sigmoid(acc + b)


def _epi_mix(ssm_b, attn_b, ga, gs, ba, bs):
    return jax.nn.sigmoid(ga + ba) * attn_b + jax.nn.sigmoid(gs + bs) * ssm_b


def _qkv_kernel(q_ref, k_ref, v_ref, gq_ref, gk_ref, qn_ref, kf_ref, kb_ref, vf_ref, vb_ref, *, hd):
    gq = gq_ref[...]
    gk = gk_ref[...]
    for j in range(q_ref.shape[1] // hd):
        sl = slice(j * hd, (j + 1) * hd)
        q = q_ref[:, sl]
        qn = q * lax.rsqrt(jnp.mean(q * q, axis=-1, keepdims=True) + EPS) * gq
        qn_ref[:, sl] = qn.astype(BF16)
        k = k_ref[:, sl]
        kn = k * lax.rsqrt(jnp.mean(k * k, axis=-1, keepdims=True) + EPS) * gk
        kf_ref[:, sl] = kn
        kb_ref[:, sl] = kn.astype(BF16)
    v = v_ref[...]
    vf_ref[...] = v
    vb_ref[...] = v.astype(BF16)


def _qkv(proj, gq, gk, qk_width, v_width, hd):
    m = proj.shape[0]
    assert qk_width == v_width
    w = qk_width
    tm = _pick(m, (256, 128, 64, 32, 16, 8))
    spec = lambda c: pl.BlockSpec((tm, w), lambda i, c=c: (i, c))
    ospec = pl.BlockSpec((tm, w), lambda i: (i, 0))
    gspec = pl.BlockSpec((1, hd), lambda i: (0, 0))
    return pl.pallas_call(
        functools.partial(_qkv_kernel, hd=hd),
        grid=(m // tm,),
        in_specs=[spec(0), spec(1), spec(2), gspec, gspec],
        out_specs=[ospec] * 5,
        out_shape=[jax.ShapeDtypeStruct((m, w), dt) for dt in (BF16, F32, BF16, F32, BF16)],
        compiler_params=_cparams(("parallel",)),
    )(proj, proj, proj, gq.reshape(1, hd).astype(F32), gk.reshape(1, hd).astype(F32))


def _rel_bucket(rel, n_buckets):
    half = n_buckets // 2
    max_exact = half // 2
    n = jnp.abs(rel)
    nf = jnp.maximum(n, 1).astype(F32)
    large = max_exact + (jnp.log(nf / max_exact) / math.log(REL_MAX_DIST / max_exact)
                         * (half - max_exact)).astype(jnp.int32)
    large = jnp.minimum(large, half - 1)
    return jnp.where(rel > 0, half, 0) + jnp.where(n < max_exact, n, large)


def _bias_table(rel_bias, q_pos, k_pos):
    rel = k_pos[None, :] - q_pos[:, None]
    bias = jnp.transpose(rel_bias.astype(F32)[_rel_bucket(rel, rel_bias.shape[0])], (2, 0, 1))
    visible = (k_pos[None, :] // CHUNK) <= (q_pos[:, None] // CHUNK)
    return jnp.where(visible[None], bias, NEG)


def _lambda(lam_ref, lam_init):
    lam = lam_ref[...]
    s1 = jnp.sum(lam[0:1] * lam[1:2], axis=-1, keepdims=True)
    s2 = jnp.sum(lam[2:3] * lam[3:4], axis=-1, keepdims=True)
    return jnp.exp(s1) - jnp.exp(s2) + lam_init


def _subln(o, g, lam_init):
    y = o * lax.rsqrt(jnp.mean(o * o, axis=-1, keepdims=True) + EPS)
    return (y * g) * (1.0 - lam_init)


_NT = (((1,), (1,)), ((), ()))


def _attn_prompt_kernel(qi_ref, kj_ref, ts_ref, q_ref, k_ref, v_ref, bias_ref, lam_ref, g_ref, o_ref,
                        m_sc, l_sc, acc_sc, *, hd, scale, lam_init):
    s = pl.program_id(2)
    qi = qi_ref[s]
    kj = kj_ref[s]

    @pl.when(kj == 0)
    def _():
        m_sc[...] = jnp.full_like(m_sc, NEG)
        l_sc[...] = jnp.zeros_like(l_sc)
        acc_sc[...] = jnp.zeros_like(acc_sc)

    q = q_ref[0]
    k = k_ref[0]
    v = v_ref[0]
    bias = bias_ref[0, 0]
    for c in range(2):
        sl = slice(c * hd, (c + 1) * hd)
        sc = lax.dot_general(q[:, sl], k[:, sl], _NT, preferred_element_type=F32) * scale + bias
        m_prev = m_sc[c]
        m_new = jnp.maximum(m_prev, jnp.max(sc, axis=-1, keepdims=True))
        alpha = jnp.exp(m_prev - m_new)
        p = jnp.exp(sc - m_new)
        l_sc[c] = alpha * l_sc[c] + jnp.sum(p, axis=-1, keepdims=True)
        acc_sc[c] = alpha * acc_sc[c] + jnp.dot(p.astype(BF16), v, preferred_element_type=F32)
        m_sc[c] = m_new

    @pl.when(kj == qi)
    def _():
        lam = _lambda(lam_ref, lam_init)
        o = acc_sc[0] / l_sc[0] - lam * (acc_sc[1] / l_sc[1])
        o_ref[0] = _subln(o, g_ref[...], lam_init).astype(o_ref.dtype)


def _prompt_block(seq):
    blk = _pick(seq, (512, 256, 128))
    assert blk >= REL_MAX_DIST and blk % CHUNK == 0
    return blk


def _prompt_tables(rel_bias, seq):
    blk = _prompt_block(seq)
    pos = jnp.arange(blk, dtype=jnp.int32)
    return jnp.stack([_bias_table(rel_bias, pos, pos),
                      _bias_table(rel_bias, pos + blk, pos),
                      _bias_table(rel_bias, pos + 2 * blk, pos)])


def _attn_prompt(qn, kb, vb, tables, lam4, g, n_heads, hd, lam_init):
    bsz, seq, width = qn.shape
    hw = 2 * hd
    blk = _prompt_block(seq)
    nq = seq // blk
    pairs = [(i, j) for i in range(nq) for j in range(i + 1)]
    qi = jnp.array([p[0] for p in pairs], jnp.int32)
    kj = jnp.array([p[1] for p in pairs], jnp.int32)
    ts = jnp.array([min(p[0] - p[1], 2) for p in pairs], jnp.int32)
    grid_spec = pltpu.PrefetchScalarGridSpec(
        num_scalar_prefetch=3,
        grid=(bsz, n_heads, len(pairs)),
        in_specs=[
            pl.BlockSpec((1, blk, hw), lambda b, h, s, qi, kj, ts: (b, qi[s], h)),
            pl.BlockSpec((1, blk, hw), lambda b, h, s, qi, kj, ts: (b, kj[s], h)),
            pl.BlockSpec((1, blk, hw), lambda b, h, s, qi, kj, ts: (b, kj[s], h)),
            pl.BlockSpec((1, 1, blk, blk), lambda b, h, s, qi, kj, ts: (ts[s], h, 0, 0)),
            pl.BlockSpec((4, hd), lambda b, h, s, qi, kj, ts: (0, 0)),
            pl.BlockSpec((1, hw), lambda b, h, s, qi, kj, ts: (0, 0)),
        ],
        out_specs=pl.BlockSpec((1, blk, hw), lambda b, h, s, qi, kj, ts: (b, qi[s], h)),
        scratch_shapes=[pltpu.VMEM((2, blk, 1), F32), pltpu.VMEM((2, blk, 1), F32),
                        pltpu.VMEM((2, blk, hw), F32)],
    )
    return pl.pallas_call(
        functools.partial(_attn_prompt_kernel, hd=hd, scale=hd ** -0.5, lam_init=lam_init),
        grid_spec=grid_spec,
        out_shape=jax.ShapeDtypeStruct((bsz, seq, width), BF16),
        compiler_params=_cparams(("parallel", "parallel", "arbitrary")),
    )(qi, kj, ts, qn, kb, vb, tables, lam4, g)


def _attn_sample_kernel(q_ref, ck_ref, cv_ref, nk_ref, nv_ref, bp_ref, bn_ref, lam_ref, g_ref, o_ref,
                        *, hd, scale, lam_init):
    q = q_ref[0]
    ck = ck_ref[0, 0].astype(BF16)
    cv = cv_ref[0, 0].astype(BF16)
    nk = nk_ref[0]
    nv = nv_ref[0]
    bp = bp_ref[0]
    bn = bn_ref[0]
    probs = []
    for c in range(2):
        sl = slice(c * hd, (c + 1) * hd)
        sp = lax.dot_general(q[:, sl], ck[:, sl], _NT, preferred_element_type=F32) * scale + bp
        sn = lax.dot_general(q[:, sl], nk[:, sl], _NT, preferred_element_type=F32) * scale + bn
        m = jnp.maximum(jnp.max(sp, axis=-1, keepdims=True), jnp.max(sn, axis=-1, keepdims=True))
        pp = jnp.exp(sp - m)
        pn = jnp.exp(sn - m)
        inv = 1.0 / (jnp.sum(pp, axis=-1, keepdims=True) + jnp.sum(pn, axis=-1, keepdims=True))
        probs.append((pp * inv, pn * inv))
    lam = _lambda(lam_ref, lam_init)
    wp = probs[0][0] - lam * probs[1][0]
    wn = probs[0][1] - lam * probs[1][1]
    o = (jnp.dot(wp.astype(BF16), cv, preferred_element_type=F32)
         + jnp.dot(wn.astype(BF16), nv, preferred_element_type=F32))
    o_ref[0] = _subln(o, g_ref[...], lam_init).astype(o_ref.dtype)


def _sample_table(rel_bias, past, s_len):
    return _bias_table(rel_bias, past + jnp.arange(s_len, dtype=jnp.int32),
                       jnp.arange(past + s_len, dtype=jnp.int32))


def _attn_sample(qn, kb, vb, cache_k, cache_v, layer, table, lam4, g, n_heads, hd, lam_init):
    bsz, s_len, width = qn.shape
    past = cache_k.shape[2]
    hw = 2 * hd
    new_spec = pl.BlockSpec((1, s_len, hw), lambda b, h: (b, 0, h))
    cache_spec = pl.BlockSpec((1, 1, past, hw), lambda b, h: (layer, b, 0, h))
    return pl.pallas_call(
        functools.partial(_attn_sample_kernel, hd=hd, scale=hd ** -0.5, lam_init=lam_init),
        grid=(bsz, n_heads),
        in_specs=[new_spec, cache_spec, cache_spec, new_spec, new_spec,
                  pl.BlockSpec((1, s_len, past), lambda b, h: (h, 0, 0)),
                  pl.BlockSpec((1, s_len, s_len), lambda b, h: (h, 0, 0)),
                  pl.BlockSpec((4, hd), lambda b, h: (0, 0)),
                  pl.BlockSpec((1, hw), lambda b, h: (0, 0))],
        out_specs=new_spec,
        out_shape=jax.ShapeDtypeStruct((bsz, s_len, width), BF16),
        compiler_params=_cparams(("parallel", "parallel")),
    )(qn, cache_k, cache_v, kb, vb, table[:, :, :past], table[:, :, past:], lam4, g)


def _split(x):
    hi = x.astype(BF16)
    lo = (x - hi.astype(F32)).astype(BF16)
    return hi, lo


def _dot3(a, b):
    ah, al = _split(a)
    bh, bl = _split(b)
    d = functools.partial(jnp.dot, preferred_element_type=F32)
    return d(ah, bh) + (d(ah, bl) + d(al, bh))


def _s5_operators(a_re, a_im, log_dt, b_re, b_im, c_re, c_im, d):
    hp = lax.Precision.HIGHEST
    g, n = a_re.shape
    c = b_re.shape[-1]
    t = S5_T
    dt = jnp.exp(log_dt.astype(F32))[:, None]
    lr = a_re.astype(F32)
    li = a_im.astype(F32)
    mag = jnp.exp(lr * dt)
    ab_re = mag * jnp.cos(li * dt)
    ab_im = mag * jnp.sin(li * dt)
    den = lr * lr + li * li
    num_re = ab_re - 1.0
    coef_re = (num_re * lr + ab_im * li) / den
    coef_im = (ab_im * lr - num_re * li) / den
    br = b_re.astype(F32)
    bi = b_im.astype(F32)
    bb_re = coef_re[..., None] * br - coef_im[..., None] * bi
    bb_im = coef_re[..., None] * bi + coef_im[..., None] * br
    pr = [jnp.ones_like(ab_re)]
    pi = [jnp.zeros_like(ab_re)]
    for _ in range(t):
        pr.append(pr[-1] * ab_re - pi[-1] * ab_im)
        pi.append(pr[-2] * ab_im + pi[-1] * ab_re)
    p_re = jnp.stack(pr)
    p_im = jnp.stack(pi)
    cr = c_re.astype(F32)
    ci = c_im.astype(F32)
    cp_re = cr[None] * p_re[:, :, None, :] - ci[None] * p_im[:, :, None, :]
    cp_im = cr[None] * p_im[:, :, None, :] + ci[None] * p_re[:, :, None, :]
    kk = (jnp.einsum("jgon,gni->gjoi", cp_re[:t], bb_re, precision=hp)
          - jnp.einsum("jgon,gni->gjoi", cp_im[:t], bb_im, precision=hp))
    lag = jnp.arange(t)[None, :] - jnp.arange(t)[:, None]
    mi = jnp.where((lag >= 0)[None, :, :, None, None], kk[:, jnp.maximum(lag, 0)], 0.0)
    mi = jnp.transpose(mi, (0, 1, 4, 2, 3)).reshape(g, t * c, t * c)
    pw_re = p_re[:t][::-1]
    pw_im = p_im[:t][::-1]
    min_re = pw_re[:, :, :, None] * bb_re[None] - pw_im[:, :, :, None] * bb_im[None]
    min_im = pw_re[:, :, :, None] * bb_im[None] + pw_im[:, :, :, None] * bb_re[None]
    min_re = jnp.transpose(min_re, (1, 0, 3, 2)).reshape(g, t * c, n)
    min_im = jnp.transpose(min_im, (1, 0, 3, 2)).reshape(g, t * c, n)
    mo_re = jnp.transpose(cp_re[1:], (1, 3, 0, 2)).reshape(g, n, t * c)
    mo_im = -jnp.transpose(cp_im[1:], (1, 3, 0, 2)).reshape(g, n, t * c)

    def pair_rows(x):
        x = x.reshape(g // 2, 2, *x.shape[1:])
        z = jnp.zeros_like(x[:, 0])
        top = jnp.concatenate([x[:, 0], z], axis=-1)
        bot = jnp.concatenate([z, x[:, 1]], axis=-1)
        return jnp.concatenate([top, bot], axis=-2)

    d_row = jnp.broadcast_to(d.astype(F32).reshape(g, 1, c), (g, t, c)).reshape(1, g * t * c)
    return dict(mi=mi, minr=pair_rows(min_re), mini=pair_rows(min_im), mor=pair_rows(mo_re),
                moi=pair_rows(mo_im), at_re=p_re[t].reshape(1, g * n), at_im=p_im[t].reshape(1, g * n),
                d_row=d_row)


def _s5_in_kernel(u_ref, mi_ref, minr_ref, mini_ref, yi_ref, vr_ref, vi_ref, *, tc):
    u = u_ref[...]
    for a in range(2):
        sl = slice(a * tc, (a + 1) * tc)
        yi_ref[:, sl] = _dot3(u[:, sl], mi_ref[a])
    vr_ref[...] = _dot3(u, minr_ref[0])
    vi_ref[...] = _dot3(u, mini_ref[0])


def _s5_scan_kernel(vr_ref, vi_ref, ar_ref, ai_ref, h0r_ref, h0i_ref, hr_ref, hi_ref, fr_ref, fi_ref, *, nchunk):
    ar = ar_ref[...]
    ai = ai_ref[...]

    def body(k, carry):
        hr, hi = carry
        hr_ref[k] = hr
        hi_ref[k] = hi
        return ar * hr - ai * hi + vr_ref[k], ar * hi + ai * hr + vi_ref[k]

    hr, hi = lax.fori_loop(0, nchunk, body, (h0r_ref[...], h0i_ref[...]))
    fr_ref[...] = hr
    fi_ref[...] = hi


def _s5_out_kernel(yi_ref, hr_ref, hi_ref, mor_ref, moi_ref, u_ref, d_ref, o_ref):
    y = yi_ref[...] + _dot3(hr_ref[...], mor_ref[0]) + _dot3(hi_ref[...], moi_ref[0]) + u_ref[...] * d_ref[...]
    o_ref[...] = jax.nn.gelu(y).astype(o_ref.dtype)


def _s5(u, h0_re, h0_im, ops, groups, n_state):
    bsz, seq, width = u.shape
    c = width // groups
    t = S5_T
    tc = t * c
    nchunk = seq // t
    rows = nchunk * bsz
    gn = groups * n_state
    u2 = jnp.transpose(u.reshape(bsz, nchunk, t, groups, c), (1, 0, 3, 2, 4)).reshape(rows, groups * tc)
    npair = groups // 2
    row_blk = lambda w: pl.BlockSpec((rows, w), lambda p: (0, p))
    yi, vr, vi = pl.pallas_call(
        functools.partial(_s5_in_kernel, tc=tc),
        grid=(npair,),
        in_specs=[row_blk(2 * tc),
                  pl.BlockSpec((2, tc, tc), lambda p: (p, 0, 0)),
                  pl.BlockSpec((1, 2 * tc, 2 * n_state), lambda p: (p, 0, 0)),
                  pl.BlockSpec((1, 2 * tc, 2 * n_state), lambda p: (p, 0, 0))],
        out_specs=[row_blk(2 * tc), row_blk(2 * n_state), row_blk(2 * n_state)],
        out_shape=[jax.ShapeDtypeStruct((rows, groups * tc), F32),
                   jax.ShapeDtypeStruct((rows, gn), F32), jax.ShapeDtypeStruct((rows, gn), F32)],
        compiler_params=_cparams(("parallel",)),
    )(u2, ops["mi"], ops["minr"], ops["mini"])

    lt = _pick(gn, (512, 256, 128))
    seq_blk = pl.BlockSpec((nchunk, bsz, lt), lambda j: (0, 0, j))
    vec_blk = pl.BlockSpec((1, lt), lambda j: (0, j))
    st_blk = pl.BlockSpec((bsz, lt), lambda j: (0, j))
    hr, hi, fr, fi = pl.pallas_call(
        functools.partial(_s5_scan_kernel, nchunk=nchunk),
        grid=(gn // lt,),
        in_specs=[seq_blk, seq_blk, vec_blk, vec_blk, st_blk, st_blk],
        out_specs=[seq_blk, seq_blk, st_blk, st_blk],
        out_shape=[jax.ShapeDtypeStruct((nchunk, bsz, gn), F32)] * 2 + [jax.ShapeDtypeStruct((bsz, gn), F32)] * 2,
        compiler_params=_cparams(("parallel",)),
    )(vr.reshape(nchunk, bsz, gn), vi.reshape(nchunk, bsz, gn), ops["at_re"], ops["at_im"], h0_re, h0_im)

    hs2 = pl.pallas_call(
        _s5_out_kernel,
        grid=(npair,),
        in_specs=[row_blk(2 * tc), row_blk(2 * n_state), row_blk(2 * n_state),
                  pl.BlockSpec((1, 2 * n_state, 2 * tc), lambda p: (p, 0, 0)),
                  pl.BlockSpec((1, 2 * n_state, 2 * tc), lambda p: (p, 0, 0)),
                  row_blk(2 * tc),
                  pl.BlockSpec((1, 2 * tc), lambda p: (0, p))],
        out_specs=row_blk(2 * tc),
        out_shape=jax.ShapeDtypeStruct((rows, groups * tc), BF16),
        compiler_params=_cparams(("parallel",)),
    )(yi, hr.reshape(rows, gn), hi.reshape(rows, gn), ops["mor"], ops["moi"], u2, ops["d_row"])
    hs = jnp.transpose(hs2.reshape(nchunk, bsz, groups, t, c), (1, 0, 3, 2, 4)).reshape(bsz, seq, width)
    return hs, fr, fi


def _router_kernel(x_ref, w_ref, b_ref, o_ref, cnt_ref, carry):
    i = pl.program_id(0)

    @pl.when(i == 0)
    def _():
        carry[...] = jnp.zeros_like(carry)

    tm = x_ref.shape[0]
    logits = jnp.dot(x_ref[...].astype(BF16), w_ref[...], preferred_element_type=F32) + b_ref[...]
    lane = lax.broadcasted_iota(jnp.int32, logits.shape, 1)
    m1 = jnp.max(logits, axis=-1, keepdims=True)
    i1 = jnp.min(jnp.where(logits == m1, lane, LANES), axis=-1, keepdims=True)
    rest = jnp.where(lane == i1, -3e38, logits)
    m2 = jnp.max(rest, axis=-1, keepdims=True)
    i2 = jnp.min(jnp.where(rest == m2, lane, LANES), axis=-1, keepdims=True)
    e = jnp.exp(m2 - m1)
    w1 = 1.0 / (1.0 + e)
    w2 = e / (1.0 + e)
    onehot = jnp.where((lane == i1) | (lane == i2), 1.0, 0.0)
    r = lax.broadcasted_iota(jnp.int32, (tm, tm), 0)
    c = lax.broadcasted_iota(jnp.int32, (tm, tm), 1)
    tri = jnp.where(c < r, 1.0, 0.0).astype(BF16)
    rank = jnp.dot(tri, onehot.astype(BF16), preferred_element_type=F32) + carry[...]
    r1 = jnp.sum(jnp.where(lane == i1, rank, 0.0), axis=-1, keepdims=True)
    r2 = jnp.sum(jnp.where(lane == i2, rank, 0.0), axis=-1, keepdims=True)
    carry[...] += jnp.sum(onehot, axis=0, keepdims=True)
    cols = (i1.astype(F32), i2.astype(F32), w1, w2, r1, r2)
    out = jnp.zeros(logits.shape, F32)
    for idx, col in enumerate(cols):
        out = jnp.where(lane == idx, col, out)
    o_ref[...] = out
    cnt_ref[...] = carry[...]


def _router(xn, w_router, b_router):
    m, d = xn.shape
    e = w_router.shape[1]
    assert e <= LANES
    tm = _pick(m, (512, 256, 128, 64, 32, 16, 8))
    w = jnp.zeros((d, LANES), BF16).at[:, :e].set(w_router.astype(BF16))
    b = jnp.full((1, LANES), NEG, F32).at[0, :e].set(b_router.astype(F32))
    return pl.pallas_call(
        _router_kernel,
        grid=(m // tm,),
        in_specs=[pl.BlockSpec((tm, d), lambda i: (i, 0)), pl.BlockSpec((d, LANES), lambda i: (0, 0)),
                  pl.BlockSpec((1, LANES), lambda i: (0, 0))],
        out_specs=[pl.BlockSpec((tm, LANES), lambda i: (i, 0)), pl.BlockSpec((1, LANES), lambda i: (0, 0))],
        out_shape=[jax.ShapeDtypeStruct((m, LANES), F32), jax.ShapeDtypeStruct((1, LANES), F32)],
        scratch_shapes=[pltpu.VMEM((1, LANES), F32)],
        compiler_params=_cparams(("arbitrary",)),
    )(xn, w, b)


def _row_copy(src, dst, src_row, dst_row, sem):
    return pltpu.make_async_copy(src.at[pl.ds(src_row, 1)], dst.at[pl.ds(dst_row, 1)], sem)


def _dispatch_kernel(pos_ref, x_hbm, zeros_hbm, xs_hbm, sem, *, tb):
    del zeros_hbm
    base = pl.program_id(0) * tb

    def issue(t, carry):
        for s in range(2):
            _row_copy(x_hbm, xs_hbm, base + t, pos_ref[2 * (base + t) + s], sem).start()
        return carry

    lax.fori_loop(0, tb, issue, 0)

    def drain(t, carry):
        for s in range(2):
            _row_copy(x_hbm, xs_hbm, 0, 0, sem).wait()
        return carry

    lax.fori_loop(0, tb, drain, 0)


def _dispatch(xn, pos, rows_padded):
    m, d = xn.shape
    tb = _pick(m, (512, 256, 128, 64, 32, 16, 8))
    grid_spec = pltpu.PrefetchScalarGridSpec(
        num_scalar_prefetch=1,
        grid=(m // tb,),
        in_specs=[pl.BlockSpec(memory_space=pl.ANY), pl.BlockSpec(memory_space=pl.ANY)],
        out_specs=pl.BlockSpec(memory_space=pl.ANY),
        scratch_shapes=[pltpu.SemaphoreType.DMA(())],
    )
    return pl.pallas_call(
        functools.partial(_dispatch_kernel, tb=tb),
        grid_spec=grid_spec,
        out_shape=jax.ShapeDtypeStruct((rows_padded, d), xn.dtype),
        input_output_aliases={2: 0},
        compiler_params=pltpu.CompilerParams(dimension_semantics=("arbitrary",), has_side_effects=True),
    )(pos, xn, jnp.zeros((rows_padded, d), xn.dtype))


def _combine_kernel(pos_ref, ys_hbm, x_ref, r_ref, o_ref, buf, sem, *, tb):
    base = pl.program_id(0) * tb

    def issue(t, carry):
        for s in range(2):
            _row_copy(ys_hbm, buf.at[s], pos_ref[2 * (base + t) + s], t, sem).start()
        return carry

    lax.fori_loop(0, tb, issue, 0)

    def drain(t, carry):
        for s in range(2):
            _row_copy(ys_hbm, buf.at[s], 0, 0, sem).wait()
        return carry

    lax.fori_loop(0, tb, drain, 0)
    r = r_ref[...]
    o_ref[...] = x_ref[...] + (r[:, 2:3] * buf[0] + r[:, 3:4] * buf[1])


def _combine(ys, pos, x, routing):
    m, d = x.shape
    tb = _pick(m, (256, 128, 64, 32, 16, 8))
    grid_spec = pltpu.PrefetchScalarGridSpec(
        num_scalar_prefetch=1,
        grid=(m // tb,),
        in_specs=[pl.BlockSpec(memory_space=pl.ANY),
                  pl.BlockSpec((tb, d), lambda i, pos: (i, 0)),
                  pl.BlockSpec((tb, LANES), lambda i, pos: (i, 0))],
        out_specs=pl.BlockSpec((tb, d), lambda i, pos: (i, 0)),
        scratch_shapes=[pltpu.VMEM((2, tb, d), F32), pltpu.SemaphoreType.DMA(())],
    )
    return pl.pallas_call(
        functools.partial(_combine_kernel, tb=tb),
        grid_spec=grid_spec,
        out_shape=jax.ShapeDtypeStruct((m, d), F32),
        compiler_params=_cparams(("arbitrary",)),
    )(pos, ys, x, routing)


def _moe(x, xn, w_router, b_router, wg, wu, wd):
    m, d = x.shape
    n_exp = wg.shape[0]
    tg = 512 if m >= 4096 else 32
    n_tiles = -(-(2 * m + n_exp * (tg - 1)) // tg)
    rows_padded = n_tiles * tg
    routing, counts = _router(xn, w_router, b_router)
    cnt = counts[0, :n_exp].astype(jnp.int32)
    padded = ((cnt + tg - 1) // tg) * tg
    ends = jnp.cumsum(padded)
    offs = ends - padded
    e1 = routing[:, 0].astype(jnp.int32)
    e2 = routing[:, 1].astype(jnp.int32)
    pos = jnp.stack([offs[e1] + routing[:, 4].astype(jnp.int32),
                     offs[e2] + routing[:, 5].astype(jnp.int32)], axis=1).reshape(-1)
    n_active = (ends[-1] // tg).reshape(1).astype(jnp.int32)
    tile_start = jnp.arange(n_tiles, dtype=jnp.int32) * tg
    tile_expert = jnp.minimum(jnp.sum(tile_start[:, None] >= ends[None, :], axis=1), n_exp - 1).astype(jnp.int32)
    xs = _dispatch(xn, pos, rows_padded)
    group = (tile_expert, n_active)
    f = wg.shape[-1]
    h = _matmul(xs, [wg, wu], _epi_swiglu, BF16, group=group, tm=tg,
                tn=_pick(f, (1408, 1024, 512, 256, 128)), tk=_pick(d, (2048, 1024, 512, 256, 128)))
    ys = _matmul(h, [wd], _epi_id, F32, group=group, tm=tg, tk=f)
    return _combine(ys, pos, x, routing)


def kernel(x_prompt, x_sample, cache_k, cache_v, state_ssm_re, state_ssm_im, rel_bias, norm_mix_g, w_in, b_gate, q_norm_g, k_norm_g, lambda_q1, lambda_k1, lambda_q2, lambda_k2, subln_g, w_attn_proj, ssm_a_re, ssm_a_im, ssm_log_dt, ssm_b_re, ssm_b_im, ssm_c_re, ssm_c_im, ssm_d, w_glu, b_glu, w_ssm_proj, w_out, norm_ffn_g, w_ff_gate, w_ff_up, w_ff_down, w_router, b_router, w_e_gate, w_e_up, w_e_down):
    bp, lp, d = x_prompt.shape
    bs, ls, _ = x_sample.shape
    depth = w_in.shape[0]
    past = cache_k.shape[2]
    n_heads, hd = cache_k.shape[3], cache_k.shape[5]
    qkw = n_heads * 2 * hd
    groups, n_state = state_ssm_re.shape[2], state_ssm_re.shape[3]
    ssm_w = ssm_d.shape[1]
    mp, ms = bp * lp, bs * ls
    assert hd % LANES == 0 and lp % S5_T == 0 and ls % S5_T == 0 and groups % 2 == 0

    ck = cache_k.reshape(depth, bs, past, qkw)
    cv = cache_v.reshape(depth, bs, past, qkw)
    x = jnp.concatenate([x_prompt.reshape(mp, d), x_sample.reshape(ms, d)], axis=0)
    m = mp + ms

    d_ff = w_ff_gate.shape[-1]
    ff_pad = (-d_ff) % 1024 if d_ff > 1024 else 0
    outs = {k: [] for k in ("kp", "vp", "hrp", "hip", "ks", "vs", "hrs", "his")}
    zeros_state = jnp.zeros((bp, groups * n_state), F32)
    tables_p = _prompt_tables(rel_bias, lp)
    table_s = _sample_table(rel_bias, past, ls)

    for l in range(depth):
        lam_init = 0.8 - 0.6 * math.exp(-0.3 * l)
        xn = _rmsnorm(x, norm_mix_g[l], BF16)
        proj = _matmul(xn, [w_in[l].astype(BF16)], _epi_id, F32)
        qn, kf, kb, vf, vb = _qkv(proj, q_norm_g[l], k_norm_g[l], qkw, qkw, hd)
        lam4 = jnp.stack([lambda_q1[l], lambda_k1[l], lambda_q2[l], lambda_k2[l]]).astype(F32)
        g_sub = subln_g[l].reshape(1, 2 * hd).astype(F32)
        r3 = lambda a, b_, l_: a.reshape(b_, l_, qkw)
        o_p = _attn_prompt(r3(qn[:mp], bp, lp), r3(kb[:mp], bp, lp), r3(vb[:mp], bp, lp),
                           tables_p, lam4, g_sub, n_heads, hd, lam_init)
        o_s = _attn_sample(r3(qn[mp:], bs, ls), r3(kb[mp:], bs, ls), r3(vb[mp:], bs, ls), ck, cv, l,
                           table_s, lam4, g_sub, n_heads, hd, lam_init)
        o = jnp.concatenate([o_p.reshape(mp, qkw), o_s.reshape(ms, qkw)], axis=0)
        attn_b = _matmul(o, [w_attn_proj[l].astype(BF16)], _epi_id, F32)

        ops = _s5_operators(ssm_a_re[l], ssm_a_im[l], ssm_log_dt[l], ssm_b_re[l], ssm_b_im[l],
                            ssm_c_re[l], ssm_c_im[l], ssm_d[l])
        u = proj[:, 3 * qkw:3 * qkw + ssm_w]
        hs_p, hr_p, hi_p = _s5(u[:mp].reshape(bp, lp, ssm_w), zeros_state, zeros_state, ops, groups, n_state)
        hs_s, hr_s, hi_s = _s5(u[mp:].reshape(bs, ls, ssm_w),
                               state_ssm_re[l].reshape(bs, -1).astype(F32),
                               state_ssm_im[l].reshape(bs, -1).astype(F32), ops, groups, n_state)
        hs = jnp.concatenate([hs_p.reshape(mp, ssm_w), hs_s.reshape(ms, ssm_w)], axis=0)
        hs = _matmul(hs, [w_glu[l].astype(BF16)], _epi_glu, BF16,
                     extras=[(hs, "tile", 0), (b_glu[l].reshape(1, -1).astype(F32), "row", 0)])
        tn = _pick(math.gcd(d, 3 * qkw + ssm_w), (1024, 512, 256, 128))
        goff = (3 * qkw + ssm_w) // tn
        bg = b_gate[l].reshape(1, -1).astype(F32)
        mix = _matmul(hs, [w_ssm_proj[l].astype(BF16)], _epi_mix, BF16, tn=tn,
                      extras=[(attn_b, "tile", 0), (proj, "tile", goff), (proj, "tile", goff + d // tn),
                              (bg, "row", 0), (bg, "row", d // tn)])
        x = _matmul(mix, [w_out[l].astype(BF16)], _epi_residual, F32, extras=[(x, "tile", 0)])

        i = l // 2
        if l % 2 == 0:
            xn = _rmsnorm(x, norm_ffn_g[l], BF16)
            wg = jnp.pad(w_ff_gate[i].astype(BF16), ((0, 0), (0, ff_pad)))
            wu = jnp.pad(w_ff_up[i].astype(BF16), ((0, 0), (0, ff_pad)))
            wd = jnp.pad(w_ff_down[i].astype(BF16), ((0, ff_pad), (0, 0)))
            h = _matmul(xn, [wg, wu], _epi_swiglu, BF16)
            x = _matmul(h, [wd], _epi_residual, F32, extras=[(x, "tile", 0)])
        else:
            xn = _rmsnorm(x, norm_ffn_g[l], F32)
            x = _moe(x, xn, w_router[i], b_router[i], w_e_gate[i].astype(BF16), w_e_up[i].astype(BF16),
                     w_e_down[i].astype(BF16))

        outs["kp"].append(kf[:mp].reshape(bp, lp, n_heads, 2, hd))
        outs["vp"].append(vf[:mp].reshape(bp, lp, n_heads, 2 * hd))
        outs["hrp"].append(hr_p.reshape(bp, groups, n_state))
        outs["hip"].append(hi_p.reshape(bp, groups, n_state))
        outs["ks"].append(kf[mp:].reshape(bs, ls, n_heads, 2, hd))
        outs["vs"].append(vf[mp:].reshape(bs, ls, n_heads, 2 * hd))
        outs["hrs"].append(hr_s.reshape(bs, groups, n_state))
        outs["his"].append(hi_s.reshape(bs, groups, n_state))

    st = {k: jnp.stack(v) for k, v in outs.items()}
    return (x[:mp].reshape(bp, lp, d), x[mp:].reshape(bs, ls, d), st["kp"], st["vp"], st["hrp"], st["hip"],
            st["ks"], st["vs"], st["hrs"], st["his"])
```

```python
import functools
import math

import jax
import jax.numpy as jnp
from jax import lax
from jax.experimental import pallas as pl
from jax.experimental.pallas import tpu as pltpu

F32 = jnp.float32
BF16 = jnp.bfloat16

CHUNK = 64
REL_MAX_DIST = 128
EPS = 1e-6
NEG = -1e30
LANES = 128
VMEM_LIMIT = 56 * 1024 * 1024
S5_T = 16
LOG2E = math.log2(math.e)


def _pick(n, cands):
    for c in cands:
        if n % c == 0:
            return c
    return n


def _cparams(sem):
    return pltpu.CompilerParams(dimension_semantics=sem, vmem_limit_bytes=VMEM_LIMIT)


def _rmsnorm_kernel(x_ref, g_ref, o_ref):
    x = x_ref[...]
    y = x * lax.rsqrt(jnp.mean(x * x, axis=-1, keepdims=True) + EPS)
    o_ref[...] = (y * g_ref[...]).astype(o_ref.dtype)


def _rmsnorm(x, g, out_dtype):
    m, d = x.shape
    tm = _pick(m, (512, 256, 128, 64, 32, 16, 8))
    return pl.pallas_call(
        _rmsnorm_kernel,
        grid=(m // tm,),
        in_specs=[pl.BlockSpec((tm, d), lambda i: (i, 0)), pl.BlockSpec((1, d), lambda i: (0, 0))],
        out_specs=pl.BlockSpec((tm, d), lambda i: (i, 0)),
        out_shape=jax.ShapeDtypeStruct((m, d), out_dtype),
        compiler_params=_cparams(("parallel",)),
    )(x, g.reshape(1, d).astype(F32))


def _mm_kernel(*refs, nb, nx, nk, epi, grouped):
    if grouped:
        nact_ref = refs[1]
        refs = refs[2:]
    a_ref = refs[0]
    b_refs = refs[1:1 + nb]
    x_refs = refs[1 + nb:1 + nb + nx]
    o_ref = refs[1 + nb + nx]
    acc_refs = refs[2 + nb + nx:]

    def body():
        a = a_ref[...].astype(BF16)
        if nk == 1:
            accs = [jnp.dot(a, b[...], preferred_element_type=F32) for b in b_refs]
            o_ref[...] = epi(*accs, *[x[...] for x in x_refs]).astype(o_ref.dtype)
            return
        k = pl.program_id(2)

        @pl.when(k == 0)
        def _():
            for acc in acc_refs:
                acc[...] = jnp.zeros_like(acc)

        for acc, b in zip(acc_refs, b_refs):
            acc[...] += jnp.dot(a, b[...], preferred_element_type=F32)

        @pl.when(k == nk - 1)
        def _():
            o_ref[...] = epi(*[acc[...] for acc in acc_refs], *[x[...] for x in x_refs]).astype(o_ref.dtype)

    if grouped:
        active = pl.program_id(0) < nact_ref[0]
        pl.when(active)(body)

        @pl.when(jnp.logical_not(active))
        def _():
            o_ref[...] = jnp.zeros_like(o_ref)
    else:
        body()


def _matmul(a, bs, epi, out_dtype, extras=(), group=None, tm=None, tn=None, tk=None):
    m, kdim = a.shape
    n = bs[0].shape[-1]
    tm = tm or _pick(m, (1024, 512, 256, 128, 64, 32, 16, 8))
    tn = tn or _pick(n, (1024, 512, 256, 128))
    tk = tk or (kdim if kdim <= 2048 else _pick(kdim, (2816, 2048, 1024, 512, 256, 128)))
    nk = kdim // tk
    grouped = group is not None

    if grouped:
        def row(i, te, na):
            return jnp.minimum(i, na[0] - 1)
        a_spec = pl.BlockSpec((tm, tk), lambda i, j, k, te, na: (row(i, te, na), k))
        b_specs = [pl.BlockSpec((None, tk, tn), lambda i, j, k, te, na: (te[row(i, te, na)], k, j)) for _ in bs]
        o_spec = pl.BlockSpec((tm, tn), lambda i, j, k, te, na: (i, j))
        x_specs = []
        assert not extras
    else:
        a_spec = pl.BlockSpec((tm, tk), lambda i, j, k: (i, k))
        b_specs = [pl.BlockSpec((tk, tn), lambda i, j, k: (k, j)) for _ in bs]
        o_spec = pl.BlockSpec((tm, tn), lambda i, j, k: (i, j))
        x_specs = []
        for arr, kind, off in extras:
            if kind == "tile":
                x_specs.append(pl.BlockSpec((tm, tn), lambda i, j, k, off=off: (i, j + off)))
            else:
                x_specs.append(pl.BlockSpec((1, tn), lambda i, j, k, off=off: (0, j + off)))

    scratch = [pltpu.VMEM((tm, tn), F32) for _ in bs] if nk > 1 else []
    kern = functools.partial(_mm_kernel, nb=len(bs), nx=len(extras), nk=nk, epi=epi, grouped=grouped)
    grid_spec = pltpu.PrefetchScalarGridSpec(
        num_scalar_prefetch=2 if grouped else 0,
        grid=(m // tm, n // tn, nk),
        in_specs=[a_spec, *b_specs, *x_specs],
        out_specs=o_spec,
        scratch_shapes=scratch,
    )
    args = ([group[0], group[1]] if grouped else []) + [a, *bs, *[e[0] for e in extras]]
    return pl.pallas_call(
        kern,
        grid_spec=grid_spec,
        out_shape=jax.ShapeDtypeStruct((m, n), out_dtype),
        compiler_params=_cparams(("arbitrary" if grouped else "parallel", "arbitrary" if grouped else "parallel",
                                  "arbitrary")),
    )(*args)


def _epi_id(acc):
    return acc


def _epi_residual(acc, x):
    return x + acc


def _epi_swiglu(g, u):
    return jax.nn.silu(g) * u


def _epi_glu(acc, hs, b):
    return hs.astype(F32) * jax.nn.sigmoid(acc + b)


def _epi_mix(ssm_b, attn_b, ga, gs, ba, bs):
    return jax.nn.sigmoid(ga + ba) * attn_b + jax.nn.sigmoid(gs + bs) * ssm_b


def _qkv_kernel(q_ref, k_ref, v_ref, gq_ref, gk_ref, qn_ref, kf_ref, kb_ref, vf_ref, vb_ref, *, hd):
    gq = gq_ref[...] * (hd ** -0.5 * LOG2E)
    gk = gk_ref[...]
    for j in range(q_ref.shape[1] // hd):
        sl = slice(j * hd, (j + 1) * hd)
        q = q_ref[:, sl]
        qn = q * lax.rsqrt(jnp.mean(q * q, axis=-1, keepdims=True) + EPS) * gq
        qn_ref[:, sl] = qn.astype(BF16)
        k = k_ref[:, sl]
        kn = k * lax.rsqrt(jnp.mean(k * k, axis=-1, keepdims=True) + EPS) * gk
        kf_ref[:, sl] = kn
        kb_ref[:, sl] = kn.astype(BF16)
    v = v_ref[...]
    vf_ref[...] = v
    vb_ref[...] = v.astype(BF16)


def _qkv(proj, gq, gk, qk_width, v_width, hd):
    m = proj.shape[0]
    assert qk_width == v_width
    w = qk_width
    tm = _pick(m, (256, 128, 64, 32, 16, 8))
    spec = lambda c: pl.BlockSpec((tm, w), lambda i, c=c: (i, c))
    ospec = pl.BlockSpec((tm, w), lambda i: (i, 0))
    gspec = pl.BlockSpec((1, hd), lambda i: (0, 0))
    return pl.pallas_call(
        functools.partial(_qkv_kernel, hd=hd),
        grid=(m // tm,),
        in_specs=[spec(0), spec(1), spec(2), gspec, gspec],
        out_specs=[ospec] * 5,
        out_shape=[jax.ShapeDtypeStruct((m, w), dt) for dt in (BF16, F32, BF16, F32, BF16)],
        compiler_params=_cparams(("parallel",)),
    )(proj, proj, proj, gq.reshape(1, hd).astype(F32), gk.reshape(1, hd).astype(F32))


def _rel_bucket(rel, n_buckets):
    half = n_buckets // 2
    max_exact = half // 2
    n = jnp.abs(rel)
    nf = jnp.maximum(n, 1).astype(F32)
    large = max_exact + (jnp.log(nf / max_exact) / math.log(REL_MAX_DIST / max_exact)
                         * (half - max_exact)).astype(jnp.int32)
    large = jnp.minimum(large, half - 1)
    return jnp.where(rel > 0, half, 0) + jnp.where(n < max_exact, n, large)


def _bias_table(rel_bias, q_pos, k_pos):
    rel = k_pos[None, :] - q_pos[:, None]
    bias = jnp.transpose(rel_bias.astype(F32)[_rel_bucket(rel, rel_bias.shape[0])], (2, 0, 1))
    visible = (k_pos[None, :] // CHUNK) <= (q_pos[:, None] // CHUNK)
    return jnp.where(visible[None], bias, NEG)


def _lambda(lam_ref, lam_init):
    lam = lam_ref[...]
    s1 = jnp.sum(lam[0:1] * lam[1:2], axis=-1, keepdims=True)
    s2 = jnp.sum(lam[2:3] * lam[3:4], axis=-1, keepdims=True)
    return jnp.exp(s1) - jnp.exp(s2) + lam_init


def _subln(o, g, lam_init):
    y = o * lax.rsqrt(jnp.mean(o * o, axis=-1, keepdims=True) + EPS)
    return (y * g) * (1.0 - lam_init)


_NT = (((1,), (1,)), ((), ()))


def _attn_prompt_kernel(qi_ref, kj_ref, ts_ref, q_ref, k_ref, v_ref, bias_ref, lam_ref, g_ref, o_ref,
                        m_sc, l_sc, acc_sc, *, hd, lam_init):
    s = pl.program_id(2)
    qi = qi_ref[s]
    kj = kj_ref[s]

    @pl.when(kj == 0)
    def _():
        m_sc[...] = jnp.full_like(m_sc, NEG)
        l_sc[...] = jnp.zeros_like(l_sc)
        acc_sc[...] = jnp.zeros_like(acc_sc)

    def step(with_bias):
        q = q_ref[...]
        k = k_ref[...]
        v = v_ref[...]
        for c in range(2):
            sl = slice(c * hd, (c + 1) * hd)
            sc = lax.dot_general(q[:, sl], k[:, sl], _NT, preferred_element_type=F32)
            if with_bias:
                sc = sc + bias_ref[0, 0]
            m_prev = m_sc[c]
            m_new = jnp.maximum(m_prev, jnp.max(sc, axis=-1, keepdims=True))
            alpha = jnp.exp2(m_prev - m_new)
            p = jnp.exp2(sc - m_new)
            l_sc[c] = alpha * l_sc[c] + jnp.sum(p, axis=-1, keepdims=True)
            acc_sc[c] = alpha * acc_sc[c] + jnp.dot(p.astype(BF16), v, preferred_element_type=F32)
            m_sc[c] = m_new

    pl.when(ts_ref[s] < 2)(functools.partial(step, True))
    pl.when(ts_ref[s] == 2)(functools.partial(step, False))

    @pl.when(kj == qi)
    def _():
        lam = _lambda(lam_ref, lam_init)
        o = acc_sc[0] / l_sc[0] - lam * (acc_sc[1] / l_sc[1])
        o_ref[...] = _subln(o, g_ref[...], lam_init).astype(o_ref.dtype)


def _prompt_block(seq):
    blk = _pick(seq, (512, 256, 128))
    assert blk >= REL_MAX_DIST and blk % CHUNK == 0
    return blk


def _prompt_tables(rel_bias, seq):
    blk = _prompt_block(seq)
    pos = jnp.arange(blk, dtype=jnp.int32)
    far = _bias_table(rel_bias, pos[:1] + 2 * blk, pos[:1])
    near = jnp.stack([_bias_table(rel_bias, pos, pos), _bias_table(rel_bias, pos + blk, pos)])
    return jnp.where(near > 0.5 * NEG, (near - far[None]) * LOG2E, NEG)


def _attn_prompt(qn, kb, vb, bsz, seq, tables, lam4, g, n_heads, hd, lam_init):
    m, width = qn.shape
    hw = 2 * hd
    blk = _prompt_block(seq)
    nq = seq // blk
    pairs = [(i, j) for i in range(nq) for j in range(i + 1)]
    qi = jnp.array([p[0] for p in pairs], jnp.int32)
    kj = jnp.array([p[1] for p in pairs], jnp.int32)
    ts = jnp.array([min(p[0] - p[1], 2) for p in pairs], jnp.int32)
    grid_spec = pltpu.PrefetchScalarGridSpec(
        num_scalar_prefetch=3,
        grid=(bsz, n_heads, len(pairs)),
        in_specs=[
            pl.BlockSpec((blk, hw), lambda b, h, s, qi, kj, ts: (b * nq + qi[s], h)),
            pl.BlockSpec((blk, hw), lambda b, h, s, qi, kj, ts: (b * nq + kj[s], h)),
            pl.BlockSpec((blk, hw), lambda b, h, s, qi, kj, ts: (b * nq + kj[s], h)),
            pl.BlockSpec((1, 1, blk, blk), lambda b, h, s, qi, kj, ts: (jnp.minimum(ts[s], 1), h, 0, 0)),
            pl.BlockSpec((4, hd), lambda b, h, s, qi, kj, ts: (0, 0)),
            pl.BlockSpec((1, hw), lambda b, h, s, qi, kj, ts: (0, 0)),
        ],
        out_specs=pl.BlockSpec((blk, hw), lambda b, h, s, qi, kj, ts: (b * nq + qi[s], h)),
        scratch_shapes=[pltpu.VMEM((2, blk, 1), F32), pltpu.VMEM((2, blk, 1), F32),
                        pltpu.VMEM((2, blk, hw), F32)],
    )
    return pl.pallas_call(
        functools.partial(_attn_prompt_kernel, hd=hd, lam_init=lam_init),
        grid_spec=grid_spec,
        out_shape=jax.ShapeDtypeStruct((m, width), BF16),
        compiler_params=_cparams(("parallel", "parallel", "arbitrary")),
    )(qi, kj, ts, qn, kb, vb, tables, lam4, g)


def _attn_sample_kernel(q_ref, ck_ref, cv_ref, nk_ref, nv_ref, bp_ref, bn_ref, lam_ref, g_ref, o_in, o_ref,
                        *, hd, lam_init):
    del o_in
    q = q_ref[...]
    ck = ck_ref[0, 0].astype(BF16)
    cv = cv_ref[0, 0].astype(BF16)
    nk = nk_ref[...]
    nv = nv_ref[...]
    bp = bp_ref[0]
    bn = bn_ref[0]
    probs = []
    for c in range(2):
        sl = slice(c * hd, (c + 1) * hd)
        sp = lax.dot_general(q[:, sl], ck[:, sl], _NT, preferred_element_type=F32) + bp
        sn = lax.dot_general(q[:, sl], nk[:, sl], _NT, preferred_element_type=F32) + bn
        m = jnp.maximum(jnp.max(sp, axis=-1, keepdims=True), jnp.max(sn, axis=-1, keepdims=True))
        pp = jnp.exp2(sp - m)
        pn = jnp.exp2(sn - m)
        inv = 1.0 / (jnp.sum(pp, axis=-1, keepdims=True) + jnp.sum(pn, axis=-1, keepdims=True))
        probs.append((pp * inv, pn * inv))
    lam = _lambda(lam_ref, lam_init)
    wp = probs[0][0] - lam * probs[1][0]
    wn = probs[0][1] - lam * probs[1][1]
    o = (jnp.dot(wp.astype(BF16), cv, preferred_element_type=F32)
         + jnp.dot(wn.astype(BF16), nv, preferred_element_type=F32))
    o_ref[...] = _subln(o, g_ref[...], lam_init).astype(o_ref.dtype)


def _sample_table(rel_bias, past, s_len):
    table = _bias_table(rel_bias, past + jnp.arange(s_len, dtype=jnp.int32),
                        jnp.arange(past + s_len, dtype=jnp.int32))
    return jnp.where(table > 0.5 * NEG, table * LOG2E, NEG)


def _attn_sample(qn, kb, vb, o, row0, bsz, s_len, cache_k, cache_v, layer, table, lam4, g, n_heads, hd, lam_init):
    past = cache_k.shape[2]
    hw = 2 * hd
    assert row0 % s_len == 0
    new_spec = pl.BlockSpec((s_len, hw), lambda b, h: (row0 // s_len + b, h))
    cache_spec = pl.BlockSpec((1, 1, past, hw), lambda b, h: (layer, b, 0, h))
    return pl.pallas_call(
        functools.partial(_attn_sample_kernel, hd=hd, lam_init=lam_init),
        grid=(bsz, n_heads),
        in_specs=[new_spec, cache_spec, cache_spec, new_spec, new_spec,
                  pl.BlockSpec((1, s_len, past), lambda b, h: (h, 0, 0)),
                  pl.BlockSpec((1, s_len, s_len), lambda b, h: (h, 0, 0)),
                  pl.BlockSpec((4, hd), lambda b, h: (0, 0)),
                  pl.BlockSpec((1, hw), lambda b, h: (0, 0)),
                  pl.BlockSpec(memory_space=pl.ANY)],
        out_specs=new_spec,
        out_shape=jax.ShapeDtypeStruct(o.shape, o.dtype),
        input_output_aliases={9: 0},
        compiler_params=_cparams(("parallel", "parallel")),
    )(qn, cache_k, cache_v, kb, vb, table[:, :, :past], table[:, :, past:], lam4, g, o)


def _split(x):
    hi = x.astype(BF16)
    lo = (x - hi.astype(F32)).astype(BF16)
    return hi, lo


def _dot3(a, b):
    ah, al = _split(a)
    bh, bl = _split(b)
    d = functools.partial(jnp.dot, preferred_element_type=F32)
    return d(ah, bh) + (d(ah, bl) + d(al, bh))


def _s5_operators(a_re, a_im, log_dt, b_re, b_im, c_re, c_im, d):
    hp = lax.Precision.HIGHEST
    g, n = a_re.shape
    c = b_re.shape[-1]
    t = S5_T
    dt = jnp.exp(log_dt.astype(F32))[:, None]
    lr = a_re.astype(F32)
    li = a_im.astype(F32)
    mag = jnp.exp(lr * dt)
    ab_re = mag * jnp.cos(li * dt)
    ab_im = mag * jnp.sin(li * dt)
    den = lr * lr + li * li
    num_re = ab_re - 1.0
    coef_re = (num_re * lr + ab_im * li) / den
    coef_im = (ab_im * lr - num_re * li) / den
    br = b_re.astype(F32)
    bi = b_im.astype(F32)
    bb_re = coef_re[..., None] * br - coef_im[..., None] * bi
    bb_im = coef_re[..., None] * bi + coef_im[..., None] * br
    pr = [jnp.ones_like(ab_re)]
    pi = [jnp.zeros_like(ab_re)]
    for _ in range(t):
        pr.append(pr[-1] * ab_re - pi[-1] * ab_im)
        pi.append(pr[-2] * ab_im + pi[-1] * ab_re)
    p_re = jnp.stack(pr)
    p_im = jnp.stack(pi)
    cr = c_re.astype(F32)
    ci = c_im.astype(F32)
    cp_re = cr[None] * p_re[:, :, None, :] - ci[None] * p_im[:, :, None, :]
    cp_im = cr[None] * p_im[:, :, None, :] + ci[None] * p_re[:, :, None, :]
    kk = (jnp.einsum("jgon,gni->gjoi", cp_re[:t], bb_re, precision=hp)
          - jnp.einsum("jgon,gni->gjoi", cp_im[:t], bb_im, precision=hp))
    oc = LANES // c
    go = g // oc
    eye = jnp.eye(oc, dtype=F32)
    lag = jnp.arange(t)[None, :] - jnp.arange(t)[:, None]
    mi = jnp.where((lag >= 0)[None, :, :, None, None], kk[:, jnp.maximum(lag, 0)], 0.0)
    mi = jnp.einsum("ogstci,gh->osgithc", mi.reshape(go, oc, t, t, c, c), eye)
    mi = mi.reshape(go, t * LANES, t * LANES).astype(BF16)
    pw_re = p_re[:t][::-1]
    pw_im = p_im[:t][::-1]
    min_re = pw_re[:, :, :, None] * bb_re[None] - pw_im[:, :, :, None] * bb_im[None]
    min_im = pw_re[:, :, :, None] * bb_im[None] + pw_im[:, :, :, None] * bb_re[None]

    def octet_in(x):
        x = jnp.transpose(x, (1, 0, 3, 2)).reshape(go, oc, t, c, n)
        return jnp.einsum("ogsin,gh->osgihn", x, eye).reshape(go, t * LANES, oc * n)

    m_in = jnp.concatenate([octet_in(min_re), octet_in(min_im)], axis=-1)

    def octet_out(x):
        x = jnp.transpose(x, (1, 3, 0, 2)).reshape(go, oc, n, t, c)
        return jnp.einsum("ognta,gh->ogntha", x, eye).reshape(go, oc * n, t * LANES)

    m_out = jnp.concatenate([octet_out(cp_re[1:]), octet_out(-cp_im[1:])], axis=1).astype(BF16)
    return dict(mi=mi, m_in=m_in, m_out=m_out, at_re=p_re[t].reshape(1, g * n), at_im=p_im[t].reshape(1, g * n),
                d_row=d.astype(F32).reshape(1, g * c))


def _gather_chunks(u_ref, xcat, t, rows):
    for s in range(t):
        xcat[:, s * LANES:(s + 1) * LANES] = u_ref[pl.ds(s, rows, stride=t), :]


def _s5_state_kernel(u_ref, min_ref, vr_ref, vi_ref, xcat, *, t, rows):
    _gather_chunks(u_ref, xcat, t, rows)
    v = _dot3(xcat[...], min_ref[0])
    half = v.shape[1] // 2
    vr_ref[...] = v[:, :half]
    vi_ref[...] = v[:, half:]


def _s5_scan_kernel(vr_ref, vi_ref, ar_ref, ai_ref, h0r_ref, h0i_ref, hr_ref, hi_ref, fr_ref, fi_ref, *, nchunk):
    ar = ar_ref[...]
    ai = ai_ref[...]

    def body(k, carry):
        hr, hi = carry
        hr_ref[pl.ds(k, 1), :] = hr
        hi_ref[pl.ds(k, 1), :] = hi
        vr = vr_ref[pl.ds(k, 1), :]
        vi = vi_ref[pl.ds(k, 1), :]
        return ar * hr - ai * hi + vr, ar * hi + ai * hr + vi

    hr, hi = lax.fori_loop(0, nchunk, body, (h0r_ref[...], h0i_ref[...]))
    fr_ref[...] = hr
    fi_ref[...] = hi


def _s5_out_kernel(u_ref, mi_ref, mo_ref, hr_ref, hi_ref, d_ref, o_in, o_ref, xcat, *, t, rows):
    del o_in
    _gather_chunks(u_ref, xcat, t, rows)
    h = jnp.concatenate([hr_ref[...], hi_ref[...]], axis=1).astype(BF16)
    y = (jnp.dot(xcat[...].astype(BF16), mi_ref[0], preferred_element_type=F32)
         + jnp.dot(h, mo_ref[0], preferred_element_type=F32))
    d = d_ref[...]
    for s in range(t):
        sl = slice(s * LANES, (s + 1) * LANES)
        o_ref[pl.ds(s, rows, stride=t), :] = jax.nn.gelu(y[:, sl] + xcat[:, sl] * d)


def _s5(proj, hs, row0, bsz, seq, ucol, h0_re, h0_im, ops, groups, n_state):
    t = S5_T
    width = ops["d_row"].shape[1]
    gn = groups * n_state
    noct = width // LANES
    ocn = gn // noct
    nchunk = seq // t
    tokens = bsz * seq
    tb = _pick(tokens, (8192, 4096, 2048, 1024, 512, 256, 128))
    rb = tb // t
    assert row0 % tb == 0 and ucol % LANES == 0 and seq % t == 0
    u_spec = pl.BlockSpec((tb, LANES), lambda o, r: (row0 // tb + r, ucol // LANES + o))
    st_spec = pl.BlockSpec((rb, ocn), lambda o, r: (r, o))
    vr, vi = pl.pallas_call(
        functools.partial(_s5_state_kernel, t=t, rows=rb),
        grid=(noct, tokens // tb),
        in_specs=[u_spec, pl.BlockSpec((1, t * LANES, 2 * ocn), lambda o, r: (o, 0, 0))],
        out_specs=[st_spec, st_spec],
        out_shape=[jax.ShapeDtypeStruct((tokens // t, gn), F32)] * 2,
        scratch_shapes=[pltpu.VMEM((rb, t * LANES), F32)],
        compiler_params=_cparams(("parallel", "parallel")),
    )(proj, ops["m_in"])

    lt = _pick(gn, (2048, 1024, 512, 256, 128))
    seq_blk = pl.BlockSpec((None, nchunk, lt), lambda b, j: (b, 0, j))
    vec_blk = pl.BlockSpec((1, lt), lambda b, j: (0, j))
    st_blk = pl.BlockSpec((None, 1, lt), lambda b, j: (b, 0, j))
    r3 = lambda x: x.reshape(bsz, nchunk, gn)
    hr, hi, fr, fi = pl.pallas_call(
        functools.partial(_s5_scan_kernel, nchunk=nchunk),
        grid=(bsz, gn // lt),
        in_specs=[seq_blk, seq_blk, vec_blk, vec_blk, st_blk, st_blk],
        out_specs=[seq_blk, seq_blk, st_blk, st_blk],
        out_shape=[jax.ShapeDtypeStruct((bsz, nchunk, gn), F32)] * 2 + [jax.ShapeDtypeStruct((bsz, 1, gn), F32)] * 2,
        compiler_params=_cparams(("parallel", "parallel")),
    )(r3(vr), r3(vi), ops["at_re"], ops["at_im"], h0_re.reshape(bsz, 1, gn), h0_im.reshape(bsz, 1, gn))

    hs = pl.pallas_call(
        functools.partial(_s5_out_kernel, t=t, rows=rb),
        grid=(noct, tokens // tb),
        in_specs=[u_spec,
                  pl.BlockSpec((1, t * LANES, t * LANES), lambda o, r: (o, 0, 0)),
                  pl.BlockSpec((1, 2 * ocn, t * LANES), lambda o, r: (o, 0, 0)),
                  st_spec, st_spec,
                  pl.BlockSpec((1, LANES), lambda o, r: (0, o)),
                  pl.BlockSpec(memory_space=pl.ANY)],
        out_specs=pl.BlockSpec((tb, LANES), lambda o, r: (row0 // tb + r, o)),
        out_shape=jax.ShapeDtypeStruct(hs.shape, hs.dtype),
        input_output_aliases={6: 0},
        scratch_shapes=[pltpu.VMEM((rb, t * LANES), F32)],
        compiler_params=_cparams(("parallel", "parallel")),
    )(proj, ops["mi"], ops["m_out"], hr.reshape(tokens // t, gn), hi.reshape(tokens // t, gn), ops["d_row"], hs)
    return hs, fr.reshape(bsz, gn), fi.reshape(bsz, gn)


def _router_kernel(x_ref, w_ref, b_ref, o_ref, cnt_ref, carry):
    i = pl.program_id(0)

    @pl.when(i == 0)
    def _():
        carry[...] = jnp.zeros_like(carry)

    tm = x_ref.shape[0]
    logits = jnp.dot(x_ref[...].astype(BF16), w_ref[...], preferred_element_type=F32) + b_ref[...]
    lane = lax.broadcasted_iota(jnp.int32, logits.shape, 1)
    m1 = jnp.max(logits, axis=-1, keepdims=True)
    i1 = jnp.min(jnp.where(logits == m1, lane, LANES), axis=-1, keepdims=True)
    rest = jnp.where(lane == i1, -3e38, logits)
    m2 = jnp.max(rest, axis=-1, keepdims=True)
    i2 = jnp.min(jnp.where(rest == m2, lane, LANES), axis=-1, keepdims=True)
    e = jnp.exp(m2 - m1)
    w1 = 1.0 / (1.0 + e)
    w2 = e / (1.0 + e)
    onehot = jnp.where((lane == i1) | (lane == i2), 1.0, 0.0)
    r = lax.broadcasted_iota(jnp.int32, (tm, tm), 0)
    c = lax.broadcasted_iota(jnp.int32, (tm, tm), 1)
    tri = jnp.where(c < r, 1.0, 0.0).astype(BF16)
    rank = jnp.dot(tri, onehot.astype(BF16), preferred_element_type=F32) + carry[...]
    r1 = jnp.sum(jnp.where(lane == i1, rank, 0.0), axis=-1, keepdims=True)
    r2 = jnp.sum(jnp.where(lane == i2, rank, 0.0), axis=-1, keepdims=True)
    carry[...] += jnp.sum(onehot, axis=0, keepdims=True)
    cols = (i1.astype(F32), i2.astype(F32), w1, w2, r1, r2)
    out = jnp.zeros(logits.shape, F32)
    for idx, col in enumerate(cols):
        out = jnp.where(lane == idx, col, out)
    o_ref[...] = out
    cnt_ref[...] = carry[...]


def _router(xn, w_router, b_router):
    m, d = xn.shape
    e = w_router.shape[1]
    assert e <= LANES
    tm = _pick(m, (512, 256, 128, 64, 32, 16, 8))
    w = jnp.zeros((d, LANES), BF16).at[:, :e].set(w_router.astype(BF16))
    b = jnp.full((1, LANES), NEG, F32).at[0, :e].set(b_router.astype(F32))
    return pl.pallas_call(
        _router_kernel,
        grid=(m // tm,),
        in_specs=[pl.BlockSpec((tm, d), lambda i: (i, 0)), pl.BlockSpec((d, LANES), lambda i: (0, 0)),
                  pl.BlockSpec((1, LANES), lambda i: (0, 0))],
        out_specs=[pl.BlockSpec((tm, LANES), lambda i: (i, 0)), pl.BlockSpec((1, LANES), lambda i: (0, 0))],
        out_shape=[jax.ShapeDtypeStruct((m, LANES), F32), jax.ShapeDtypeStruct((1, LANES), F32)],
        scratch_shapes=[pltpu.VMEM((1, LANES), F32)],
        compiler_params=_cparams(("arbitrary",)),
    )(xn, w, b)


def _row_copy(src, dst, src_row, dst_row, sem):
    return pltpu.make_async_copy(src.at[pl.ds(src_row, 1)], dst.at[pl.ds(dst_row, 1)], sem)


def _dispatch_kernel(pos_ref, x_ref, zeros_hbm, xs_hbm, sem, *, tb):
    del zeros_hbm
    base = pl.program_id(0) * tb

    def issue(t, carry):
        for s in range(2):
            _row_copy(x_ref, xs_hbm, t, pos_ref[2 * (base + t) + s], sem).start()
        return carry

    lax.fori_loop(0, tb, issue, 0)

    def drain(t, carry):
        for s in range(2):
            _row_copy(x_ref, xs_hbm, 0, 0, sem).wait()
        return carry

    lax.fori_loop(0, tb, drain, 0)


def _dispatch(xn, pos, rows_padded):
    m, d = xn.shape
    tb = _pick(m, (512, 256, 128, 64, 32, 16, 8))
    grid_spec = pltpu.PrefetchScalarGridSpec(
        num_scalar_prefetch=1,
        grid=(m // tb,),
        in_specs=[pl.BlockSpec((tb, d), lambda i, pos: (i, 0)), pl.BlockSpec(memory_space=pl.ANY)],
        out_specs=pl.BlockSpec(memory_space=pl.ANY),
        scratch_shapes=[pltpu.SemaphoreType.DMA(())],
    )
    return pl.pallas_call(
        functools.partial(_dispatch_kernel, tb=tb),
        grid_spec=grid_spec,
        out_shape=jax.ShapeDtypeStruct((rows_padded, d), xn.dtype),
        input_output_aliases={2: 0},
        compiler_params=pltpu.CompilerParams(dimension_semantics=("arbitrary",), has_side_effects=True,
                                             vmem_limit_bytes=VMEM_LIMIT),
    )(pos, xn, jnp.zeros((rows_padded, d), xn.dtype))


def _combine_kernel(pos_ref, ys_hbm, x_ref, r_ref, o_ref, buf, sem, *, tb):
    base = pl.program_id(0) * tb

    def issue(t, carry):
        for s in range(2):
            _row_copy(ys_hbm, buf.at[s], pos_ref[2 * (base + t) + s], t, sem).start()
        return carry

    lax.fori_loop(0, tb, issue, 0)

    def drain(t, carry):
        for s in range(2):
            _row_copy(ys_hbm, buf.at[s], 0, 0, sem).wait()
        return carry

    lax.fori_loop(0, tb, drain, 0)
    r = r_ref[...]
    o_ref[...] = x_ref[...] + (r[:, 2:3] * buf[0] + r[:, 3:4] * buf[1])


def _combine(ys, pos, x, routing):
    m, d = x.shape
    tb = _pick(m, (256, 128, 64, 32, 16, 8))
    grid_spec = pltpu.PrefetchScalarGridSpec(
        num_scalar_prefetch=1,
        grid=(m // tb,),
        in_specs=[pl.BlockSpec(memory_space=pl.ANY),
                  pl.BlockSpec((tb, d), lambda i, pos: (i, 0)),
                  pl.BlockSpec((tb, LANES), lambda i, pos: (i, 0))],
        out_specs=pl.BlockSpec((tb, d), lambda i, pos: (i, 0)),
        scratch_shapes=[pltpu.VMEM((2, tb, d), F32), pltpu.SemaphoreType.DMA(())],
    )
    return pl.pallas_call(
        functools.partial(_combine_kernel, tb=tb),
        grid_spec=grid_spec,
        out_shape=jax.ShapeDtypeStruct((m, d), F32),
        compiler_params=_cparams(("arbitrary",)),
    )(pos, ys, x, routing)


def _moe(x, xn, w_router, b_router, wg, wu, wd):
    m, d = x.shape
    n_exp = wg.shape[0]
    tg = 512 if m >= 4096 else 32
    n_tiles = -(-(2 * m + n_exp * (tg - 1)) // tg)
    rows_padded = n_tiles * tg
    routing, counts = _router(xn, w_router, b_router)
    cnt = counts[0, :n_exp].astype(jnp.int32)
    padded = ((cnt + tg - 1) // tg) * tg
    ends = jnp.cumsum(padded)
    offs = ends - padded
    e1 = routing[:, 0].astype(jnp.int32)
    e2 = routing[:, 1].astype(jnp.int32)
    pos = jnp.stack([offs[e1] + routing[:, 4].astype(jnp.int32),
                     offs[e2] + routing[:, 5].astype(jnp.int32)], axis=1).reshape(-1)
    n_active = (ends[-1] // tg).reshape(1).astype(jnp.int32)
    tile_start = jnp.arange(n_tiles, dtype=jnp.int32) * tg
    tile_expert = jnp.minimum(jnp.sum(tile_start[:, None] >= ends[None, :], axis=1), n_exp - 1).astype(jnp.int32)
    xs = _dispatch(xn, pos, rows_padded)
    group = (tile_expert, n_active)
    f = wg.shape[-1]
    h = _matmul(xs, [wg, wu], _epi_swiglu, BF16, group=group, tm=tg,
                tn=_pick(f, (1408, 1024, 512, 256, 128)), tk=_pick(d, (2048, 1024, 512, 256, 128)))
    ys = _matmul(h, [wd], _epi_id, F32, group=group, tm=tg, tk=f)
    return _combine(ys, pos, x, routing)


def kernel(x_prompt, x_sample, cache_k, cache_v, state_ssm_re, state_ssm_im, rel_bias, norm_mix_g, w_in, b_gate, q_norm_g, k_norm_g, lambda_q1, lambda_k1, lambda_q2, lambda_k2, subln_g, w_attn_proj, ssm_a_re, ssm_a_im, ssm_log_dt, ssm_b_re, ssm_b_im, ssm_c_re, ssm_c_im, ssm_d, w_glu, b_glu, w_ssm_proj, w_out, norm_ffn_g, w_ff_gate, w_ff_up, w_ff_down, w_router, b_router, w_e_gate, w_e_up, w_e_down):
    bp, lp, d = x_prompt.shape
    bs, ls, _ = x_sample.shape
    depth = w_in.shape[0]
    past = cache_k.shape[2]
    n_heads, hd = cache_k.shape[3], cache_k.shape[5]
    qkw = n_heads * 2 * hd
    groups, n_state = state_ssm_re.shape[2], state_ssm_re.shape[3]
    ssm_w = ssm_d.shape[1]
    mp, ms = bp * lp, bs * ls
    assert hd % LANES == 0 and lp % S5_T == 0 and ls % S5_T == 0 and ssm_w % LANES == 0

    ck = cache_k.reshape(depth, bs, past, qkw)
    cv = cache_v.reshape(depth, bs, past, qkw)
    x = jnp.concatenate([x_prompt.reshape(mp, d), x_sample.reshape(ms, d)], axis=0)
    m = mp + ms

    d_ff = w_ff_gate.shape[-1]
    ff_pad = (-d_ff) % 1024 if d_ff > 1024 else 0
    outs = {k: [] for k in ("kp", "vp", "hrp", "hip", "ks", "vs", "hrs", "his")}
    zeros_state = jnp.zeros((bp, groups * n_state), F32)
    tables_p = _prompt_tables(rel_bias, lp)
    table_s = _sample_table(rel_bias, past, ls)

    for l in range(depth):
        lam_init = 0.8 - 0.6 * math.exp(-0.3 * l)
        xn = _rmsnorm(x, norm_mix_g[l], BF16)
        proj = _matmul(xn, [w_in[l].astype(BF16)], _epi_id, F32)
        qn, kf, kb, vf, vb = _qkv(proj, q_norm_g[l], k_norm_g[l], qkw, qkw, hd)
        lam4 = jnp.stack([lambda_q1[l], lambda_k1[l], lambda_q2[l], lambda_k2[l]]).astype(F32)
        g_sub = subln_g[l].reshape(1, 2 * hd).astype(F32)
        o = _attn_prompt(qn, kb, vb, bp, lp, tables_p, lam4, g_sub, n_heads, hd, lam_init)
        o = _attn_sample(qn, kb, vb, o, mp, bs, ls, ck, cv, l, table_s, lam4, g_sub, n_heads, hd, lam_init)
        attn_b = _matmul(o, [w_attn_proj[l].astype(BF16)], _epi_id, F32)

        ops = _s5_operators(ssm_a_re[l], ssm_a_im[l], ssm_log_dt[l], ssm_b_re[l], ssm_b_im[l],
                            ssm_c_re[l], ssm_c_im[l], ssm_d[l])
        hs = jnp.zeros((m, ssm_w), F32)
        hs, hr_p, hi_p = _s5(proj, hs, 0, bp, lp, 3 * qkw, zeros_state, zeros_state, ops, groups, n_state)
        hs, hr_s, hi_s = _s5(proj, hs, mp, bs, ls, 3 * qkw,
                             state_ssm_re[l].reshape(bs, -1).astype(F32),
                             state_ssm_im[l].reshape(bs, -1).astype(F32), ops, groups, n_state)
        hs = _matmul(hs, [w_glu[l].astype(BF16)], _epi_glu, BF16,
                     extras=[(hs, "tile", 0), (b_glu[l].reshape(1, -1).astype(F32), "row", 0)])
        tn = _pick(math.gcd(d, 3 * qkw + ssm_w), (1024, 512, 256, 128))
        goff = (3 * qkw + ssm_w) // tn
        bg = b_gate[l].reshape(1, -1).astype(F32)
        mix = _matmul(hs, [w_ssm_proj[l].astype(BF16)], _epi_mix, BF16, tn=tn,
                      extras=[(attn_b, "tile", 0), (proj, "tile", goff), (proj, "tile", goff + d // tn),
                              (bg, "row", 0), (bg, "row", d // tn)])
        x = _matmul(mix, [w_out[l].astype(BF16)], _epi_residual, F32, extras=[(x, "tile", 0)])

        i = l // 2
        if l % 2 == 0:
            xn = _rmsnorm(x, norm_ffn_g[l], BF16)
            wg = jnp.pad(w_ff_gate[i].astype(BF16), ((0, 0), (0, ff_pad)))
            wu = jnp.pad(w_ff_up[i].astype(BF16), ((0, 0), (0, ff_pad)))
            wd = jnp.pad(w_ff_down[i].astype(BF16), ((0, ff_pad), (0, 0)))
            h = _matmul(xn, [wg, wu], _epi_swiglu, BF16)
            x = _matmul(h, [wd], _epi_residual, F32, extras=[(x, "tile", 0)])
        else:
            xn = _rmsnorm(x, norm_ffn_g[l], F32)
            x = _moe(x, xn, w_router[i], b_router[i], w_e_gate[i].astype(BF16), w_e_up[i].astype(BF16),
                     w_e_down[i].astype(BF16))

        outs["kp"].append(kf[:mp].reshape(bp, lp, n_heads, 2, hd))
        outs["vp"].append(vf[:mp].reshape(bp, lp, n_heads, 2 * hd))
        outs["hrp"].append(hr_p.reshape(bp, groups, n_state))
        outs["hip"].append(hi_p.reshape(bp, groups, n_state))
        outs["ks"].append(kf[mp:].reshape(bs, ls, n_heads, 2, hd))
        outs["vs"].append(vf[mp:].reshape(bs, ls, n_heads, 2 * hd))
        outs["hrs"].append(hr_s.reshape(bs, groups, n_state))
        outs["his"].append(hi_s.reshape(bs, groups, n_state))

    st = {k: jnp.stack(v) for k, v in outs.items()}
    return (x[:mp].reshape(bp, lp, d), x[mp:].reshape(bs, ls, d), st["kp"], st["vp"], st["hrp"], st["hip"],
            st["ks"], st["vs"], st["hrs"], st["his"])
```

```python
import functools
import math

import jax
import jax.numpy as jnp
from jax import lax
from jax.experimental import pallas as pl
from jax.experimental.pallas import tpu as pltpu

F32 = jnp.float32
BF16 = jnp.bfloat16

CHUNK = 64
REL_MAX_DIST = 128
EPS = 1e-6
NEG = -1e30
LANES = 128
VMEM_LIMIT = 56 * 1024 * 1024
MM_VMEM_BUDGET = 46 * 1024 * 1024
S5_T = 16
LOG2E = math.log2(math.e)


def _pick(n, cands):
    for c in cands:
        if n % c == 0:
            return c
    return n


def _cparams(sem):
    return pltpu.CompilerParams(dimension_semantics=sem, vmem_limit_bytes=VMEM_LIMIT)


def _rmsnorm_kernel(x_ref, g_ref, o_ref):
    x = x_ref[...]
    y = x * lax.rsqrt(jnp.mean(x * x, axis=-1, keepdims=True) + EPS)
    o_ref[...] = (y * g_ref[...]).astype(o_ref.dtype)


def _rmsnorm(x, g, out_dtype):
    m, d = x.shape
    tm = _pick(m, (512, 256, 128, 64, 32, 16, 8))
    return pl.pallas_call(
        _rmsnorm_kernel,
        grid=(m // tm,),
        in_specs=[pl.BlockSpec((tm, d), lambda i: (i, 0)), pl.BlockSpec((1, d), lambda i: (0, 0))],
        out_specs=pl.BlockSpec((tm, d), lambda i: (i, 0)),
        out_shape=jax.ShapeDtypeStruct((m, d), out_dtype),
        compiler_params=_cparams(("parallel",)),
    )(x, g.reshape(1, d).astype(F32))


def _mm_kernel(*refs, nb, nx, nk, epi, grouped):
    if grouped:
        nact_ref = refs[1]
        refs = refs[2:]
    a_ref = refs[0]
    b_refs = refs[1:1 + nb]
    x_refs = refs[1 + nb:1 + nb + nx]
    o_ref = refs[1 + nb + nx]
    acc_refs = refs[2 + nb + nx:]

    def body():
        a = a_ref[...].astype(BF16)
        if nk == 1:
            accs = [jnp.dot(a, b[...], preferred_element_type=F32) for b in b_refs]
            o_ref[...] = epi(*accs, *[x[...] for x in x_refs]).astype(o_ref.dtype)
            return
        k = pl.program_id(2)

        @pl.when(k == 0)
        def _():
            for acc in acc_refs:
                acc[...] = jnp.zeros_like(acc)

        for acc, b in zip(acc_refs, b_refs):
            acc[...] += jnp.dot(a, b[...], preferred_element_type=F32)

        @pl.when(k == nk - 1)
        def _():
            o_ref[...] = epi(*[acc[...] for acc in acc_refs], *[x[...] for x in x_refs]).astype(o_ref.dtype)

    if grouped:
        active = pl.program_id(0) < nact_ref[0]
        pl.when(active)(body)

        @pl.when(jnp.logical_not(active))
        def _():
            o_ref[...] = jnp.zeros_like(o_ref)
    else:
        body()


def _matmul(a, bs, epi, out_dtype, extras=(), group=None, tm=None, tn=None, tk=None):
    m, kdim = a.shape
    n = bs[0].shape[-1]
    tm = tm or _pick(m, (1024, 512, 256, 128, 64, 32, 16, 8))
    tn = tn or _pick(n, (1024, 512, 256, 128) if len(bs) == 1 else (512, 256, 128))
    if tk is None:
        a_bytes = a.dtype.itemsize
        fixed = (2 * tm * tn * jnp.dtype(out_dtype).itemsize + len(bs) * tm * tn * 4
                 + sum(2 * (tm if kind == "tile" else 1) * tn * arr.dtype.itemsize for arr, kind, _ in extras))
        per_k = 2 * (tm * a_bytes + len(bs) * tn * 2) + (tm * 2 if a_bytes > 2 else 0)
        tk = next((c for c in (kdim, 2816, 2048, 1024, 512, 256) if kdim % c == 0 and fixed + c * per_k <= MM_VMEM_BUDGET),
                  LANES)
    nk = kdim // tk
    grouped = group is not None

    if grouped:
        def row(i, te, na):
            return jnp.minimum(i, na[0] - 1)
        a_spec = pl.BlockSpec((tm, tk), lambda i, j, k, te, na: (row(i, te, na), k))
        b_specs = [pl.BlockSpec((None, tk, tn), lambda i, j, k, te, na: (te[row(i, te, na)], k, j)) for _ in bs]
        o_spec = pl.BlockSpec((tm, tn), lambda i, j, k, te, na: (i, j))
        x_specs = []
        assert not extras
    else:
        a_spec = pl.BlockSpec((tm, tk), lambda i, j, k: (i, k))
        b_specs = [pl.BlockSpec((tk, tn), lambda i, j, k: (k, j)) for _ in bs]
        o_spec = pl.BlockSpec((tm, tn), lambda i, j, k: (i, j))
        x_specs = []
        for arr, kind, off in extras:
            if kind == "tile":
                x_specs.append(pl.BlockSpec((tm, tn), lambda i, j, k, off=off: (i, j + off)))
            else:
                x_specs.append(pl.BlockSpec((1, tn), lambda i, j, k, off=off: (0, j + off)))

    scratch = [pltpu.VMEM((tm, tn), F32) for _ in bs] if nk > 1 else []
    kern = functools.partial(_mm_kernel, nb=len(bs), nx=len(extras), nk=nk, epi=epi, grouped=grouped)
    grid_spec = pltpu.PrefetchScalarGridSpec(
        num_scalar_prefetch=2 if grouped else 0,
        grid=(m // tm, n // tn, nk),
        in_specs=[a_spec, *b_specs, *x_specs],
        out_specs=o_spec,
        scratch_shapes=scratch,
    )
    args = ([group[0], group[1]] if grouped else []) + [a, *bs, *[e[0] for e in extras]]
    return pl.pallas_call(
        kern,
        grid_spec=grid_spec,
        out_shape=jax.ShapeDtypeStruct((m, n), out_dtype),
        compiler_params=_cparams(("arbitrary" if grouped else "parallel", "arbitrary" if grouped else "parallel",
                                  "arbitrary")),
    )(*args)


def _epi_id(acc):
    return acc


def _epi_residual(acc, x):
    return x + acc


def _epi_swiglu(g, u):
    return jax.nn.silu(g) * u


def _epi_glu(acc, hs, b):
    return hs.astype(F32) * jax.nn.sigmoid(acc + b)


def _epi_mix(ssm_b, attn_b, ga, gs, ba, bs):
    return jax.nn.sigmoid(ga + ba) * attn_b + jax.nn.sigmoid(gs + bs) * ssm_b


def _qkv_kernel(q_ref, k_ref, v_ref, gq_ref, gk_ref, qn_ref, kf_ref, kb_ref, vf_ref, vb_ref, *, hd):
    gq = gq_ref[...] * (hd ** -0.5 * LOG2E)
    gk = gk_ref[...]
    for j in range(q_ref.shape[1] // hd):
        sl = slice(j * hd, (j + 1) * hd)
        q = q_ref[:, sl]
        qn = q * lax.rsqrt(jnp.mean(q * q, axis=-1, keepdims=True) + EPS) * gq
        qn_ref[:, sl] = qn.astype(BF16)
        k = k_ref[:, sl]
        kn = k * lax.rsqrt(jnp.mean(k * k, axis=-1, keepdims=True) + EPS) * gk
        kf_ref[:, sl] = kn
        kb_ref[:, sl] = kn.astype(BF16)
    v = v_ref[...]
    vf_ref[...] = v
    vb_ref[...] = v.astype(BF16)


def _qkv(proj, gq, gk, qk_width, v_width, hd):
    m = proj.shape[0]
    assert qk_width == v_width
    w = qk_width
    tm = _pick(m, (256, 128, 64, 32, 16, 8))
    spec = lambda c: pl.BlockSpec((tm, w), lambda i, c=c: (i, c))
    ospec = pl.BlockSpec((tm, w), lambda i: (i, 0))
    gspec = pl.BlockSpec((1, hd), lambda i: (0, 0))
    return pl.pallas_call(
        functools.partial(_qkv_kernel, hd=hd),
        grid=(m // tm,),
        in_specs=[spec(0), spec(1), spec(2), gspec, gspec],
        out_specs=[ospec] * 5,
        out_shape=[jax.ShapeDtypeStruct((m, w), dt) for dt in (BF16, F32, BF16, F32, BF16)],
        compiler_params=_cparams(("parallel",)),
    )(proj, proj, proj, gq.reshape(1, hd).astype(F32), gk.reshape(1, hd).astype(F32))


def _rel_bucket(rel, n_buckets):
    half = n_buckets // 2
    max_exact = half // 2
    n = jnp.abs(rel)
    nf = jnp.maximum(n, 1).astype(F32)
    large = max_exact + (jnp.log(nf / max_exact) / math.log(REL_MAX_DIST / max_exact)
                         * (half - max_exact)).astype(jnp.int32)
    large = jnp.minimum(large, half - 1)
    return jnp.where(rel > 0, half, 0) + jnp.where(n < max_exact, n, large)


def _bias_table(rel_bias, q_pos, k_pos):
    rel = k_pos[None, :] - q_pos[:, None]
    bias = jnp.transpose(rel_bias.astype(F32)[_rel_bucket(rel, rel_bias.shape[0])], (2, 0, 1))
    visible = (k_pos[None, :] // CHUNK) <= (q_pos[:, None] // CHUNK)
    return jnp.where(visible[None], bias, NEG)


def _lambda(lam_ref, lam_init):
    lam = lam_ref[...]
    s1 = jnp.sum(lam[0:1] * lam[1:2], axis=-1, keepdims=True)
    s2 = jnp.sum(lam[2:3] * lam[3:4], axis=-1, keepdims=True)
    return jnp.exp(s1) - jnp.exp(s2) + lam_init


def _subln(o, g, lam_init):
    y = o * lax.rsqrt(jnp.mean(o * o, axis=-1, keepdims=True) + EPS)
    return (y * g) * (1.0 - lam_init)


_NT = (((1,), (1,)), ((), ()))


def _attn_prompt_kernel(qi_ref, kj_ref, ts_ref, q_ref, k_ref, v_ref, bias_ref, lam_ref, g_ref, o_ref,
                        m_sc, acc_sc, *, hd, lam_init):
    s = pl.program_id(2)
    qi = qi_ref[s]
    kj = kj_ref[s]

    @pl.when(kj == 0)
    def _():
        m_sc[...] = jnp.full_like(m_sc, NEG)
        acc_sc[...] = jnp.zeros_like(acc_sc)

    def step(with_bias):
        q = q_ref[...]
        k = k_ref[...]
        v = jnp.concatenate([v_ref[...], jnp.ones((k.shape[0], LANES), BF16)], axis=1)
        for c in range(2):
            sl = slice(c * hd, (c + 1) * hd)
            sc = lax.dot_general(q[:, sl], k[:, sl], _NT, preferred_element_type=F32)
            if with_bias:
                sc = sc + bias_ref[0, 0]
            m_prev = m_sc[c]
            m_new = jnp.maximum(m_prev, jnp.max(sc, axis=-1, keepdims=True))
            alpha = jnp.exp2(m_prev - m_new)
            p = jnp.exp2(sc - jnp.concatenate([m_new] * (sc.shape[1] // LANES), axis=1))
            acc_sc[c] = (jnp.concatenate([alpha] * (acc_sc.shape[2] // LANES), axis=1) * acc_sc[c]
                         + jnp.dot(p.astype(BF16), v, preferred_element_type=F32))
            m_sc[c] = m_new

    pl.when(ts_ref[s] < 2)(functools.partial(step, True))
    pl.when(ts_ref[s] == 2)(functools.partial(step, False))

    @pl.when(kj == qi)
    def _():
        lam = _lambda(lam_ref, lam_init)
        hw = 2 * hd
        o = []
        for c in range(2):
            acc = acc_sc[c]
            o.append(acc[:, :hw] / jnp.concatenate([acc[:, hw:]] * (hw // LANES), axis=1))
        o_ref[...] = _subln(o[0] - lam * o[1], g_ref[...], lam_init).astype(o_ref.dtype)


def _prompt_block(seq):
    blk = _pick(seq, (512, 256, 128))
    assert blk >= REL_MAX_DIST and blk % CHUNK == 0
    return blk


def _prompt_tables(rel_bias, seq):
    blk = _prompt_block(seq)
    pos = jnp.arange(blk, dtype=jnp.int32)
    far = _bias_table(rel_bias, pos[:1] + 2 * blk, pos[:1])
    near = jnp.stack([_bias_table(rel_bias, pos, pos), _bias_table(rel_bias, pos + blk, pos)])
    return jnp.where(near > 0.5 * NEG, (near - far[None]) * LOG2E, NEG)


def _attn_prompt(qn, kb, vb, bsz, seq, tables, lam4, g, n_heads, hd, lam_init):
    m, width = qn.shape
    hw = 2 * hd
    blk = _prompt_block(seq)
    nq = seq // blk
    pairs = [(i, j) for i in range(nq) for j in range(i + 1)]
    qi = jnp.array([p[0] for p in pairs], jnp.int32)
    kj = jnp.array([p[1] for p in pairs], jnp.int32)
    ts = jnp.array([min(p[0] - p[1], 2) for p in pairs], jnp.int32)
    grid_spec = pltpu.PrefetchScalarGridSpec(
        num_scalar_prefetch=3,
        grid=(bsz, n_heads, len(pairs)),
        in_specs=[
            pl.BlockSpec((blk, hw), lambda b, h, s, qi, kj, ts: (b * nq + qi[s], h)),
            pl.BlockSpec((blk, hw), lambda b, h, s, qi, kj, ts: (b * nq + kj[s], h)),
            pl.BlockSpec((blk, hw), lambda b, h, s, qi, kj, ts: (b * nq + kj[s], h)),
            pl.BlockSpec((1, 1, blk, blk), lambda b, h, s, qi, kj, ts: (jnp.minimum(ts[s], 1), h, 0, 0)),
            pl.BlockSpec((4, hd), lambda b, h, s, qi, kj, ts: (0, 0)),
            pl.BlockSpec((1, hw), lambda b, h, s, qi, kj, ts: (0, 0)),
        ],
        out_specs=pl.BlockSpec((blk, hw), lambda b, h, s, qi, kj, ts: (b * nq + qi[s], h)),
        scratch_shapes=[pltpu.VMEM((2, blk, LANES), F32),
                        pltpu.VMEM((2, blk, hw + LANES), F32)],
    )
    return pl.pallas_call(
        functools.partial(_attn_prompt_kernel, hd=hd, lam_init=lam_init),
        grid_spec=grid_spec,
        out_shape=jax.ShapeDtypeStruct((m, width), BF16),
        compiler_params=_cparams(("parallel", "parallel", "arbitrary")),
    )(qi, kj, ts, qn, kb, vb, tables, lam4, g)


def _attn_sample_kernel(q_ref, ck_ref, cv_ref, nk_ref, nv_ref, bp_ref, bn_ref, lam_ref, g_ref, o_in, o_ref,
                        *, hd, lam_init):
    del o_in
    q = q_ref[...]
    ck = ck_ref[0, 0].astype(BF16)
    cv = cv_ref[0, 0].astype(BF16)
    nk = nk_ref[...]
    nv = nv_ref[...]
    bp = bp_ref[0]
    bn = bn_ref[0]
    probs = []
    for c in range(2):
        sl = slice(c * hd, (c + 1) * hd)
        sp = lax.dot_general(q[:, sl], ck[:, sl], _NT, preferred_element_type=F32) + bp
        sn = lax.dot_general(q[:, sl], nk[:, sl], _NT, preferred_element_type=F32) + bn
        m = jnp.maximum(jnp.max(sp, axis=-1, keepdims=True), jnp.max(sn, axis=-1, keepdims=True))
        pp = jnp.exp2(sp - m)
        pn = jnp.exp2(sn - m)
        inv = 1.0 / (jnp.sum(pp, axis=-1, keepdims=True) + jnp.sum(pn, axis=-1, keepdims=True))
        probs.append((pp * inv, pn * inv))
    lam = _lambda(lam_ref, lam_init)
    wp = probs[0][0] - lam * probs[1][0]
    wn = probs[0][1] - lam * probs[1][1]
    o = (jnp.dot(wp.astype(BF16), cv, preferred_element_type=F32)
         + jnp.dot(wn.astype(BF16), nv, preferred_element_type=F32))
    o_ref[...] = _subln(o, g_ref[...], lam_init).astype(o_ref.dtype)


def _sample_table(rel_bias, past, s_len):
    table = _bias_table(rel_bias, past + jnp.arange(s_len, dtype=jnp.int32),
                        jnp.arange(past + s_len, dtype=jnp.int32))
    return jnp.where(table > 0.5 * NEG, table * LOG2E, NEG)


def _attn_sample(qn, kb, vb, o, row0, bsz, s_len, cache_k, cache_v, layer, table, lam4, g, n_heads, hd, lam_init):
    past = cache_k.shape[2]
    hw = 2 * hd
    assert row0 % s_len == 0
    new_spec = pl.BlockSpec((s_len, hw), lambda b, h: (row0 // s_len + b, h))
    cache_spec = pl.BlockSpec((1, 1, past, hw), lambda b, h: (layer, b, 0, h))
    return pl.pallas_call(
        functools.partial(_attn_sample_kernel, hd=hd, lam_init=lam_init),
        grid=(bsz, n_heads),
        in_specs=[new_spec, cache_spec, cache_spec, new_spec, new_spec,
                  pl.BlockSpec((1, s_len, past), lambda b, h: (h, 0, 0)),
                  pl.BlockSpec((1, s_len, s_len), lambda b, h: (h, 0, 0)),
                  pl.BlockSpec((4, hd), lambda b, h: (0, 0)),
                  pl.BlockSpec((1, hw), lambda b, h: (0, 0)),
                  pl.BlockSpec(memory_space=pl.ANY)],
        out_specs=new_spec,
        out_shape=jax.ShapeDtypeStruct(o.shape, o.dtype),
        input_output_aliases={9: 0},
        compiler_params=_cparams(("parallel", "parallel")),
    )(qn, cache_k, cache_v, kb, vb, table[:, :, :past], table[:, :, past:], lam4, g, o)


def _split(x):
    hi = x.astype(BF16)
    lo = (x - hi.astype(F32)).astype(BF16)
    return hi, lo


def _dot3(a, b):
    ah, al = _split(a)
    bh, bl = _split(b)
    d = functools.partial(jnp.dot, preferred_element_type=F32)
    return d(ah, bh) + (d(ah, bl) + d(al, bh))


def _s5_operators(a_re, a_im, log_dt, b_re, b_im, c_re, c_im, d):
    hp = lax.Precision.HIGHEST
    g, n = a_re.shape
    c = b_re.shape[-1]
    t = S5_T
    assert 2 * n == LANES and LANES % c == 0
    dt = jnp.exp(log_dt.astype(F32))[:, None]
    lr = a_re.astype(F32)
    li = a_im.astype(F32)
    mag = jnp.exp(lr * dt)
    ab_re = mag * jnp.cos(li * dt)
    ab_im = mag * jnp.sin(li * dt)
    den = lr * lr + li * li
    num_re = ab_re - 1.0
    coef_re = (num_re * lr + ab_im * li) / den
    coef_im = (ab_im * lr - num_re * li) / den
    br = b_re.astype(F32)
    bi = b_im.astype(F32)
    bb_re = coef_re[..., None] * br - coef_im[..., None] * bi
    bb_im = coef_re[..., None] * bi + coef_im[..., None] * br
    pr = [jnp.ones_like(ab_re)]
    pi = [jnp.zeros_like(ab_re)]
    for _ in range(t):
        pr.append(pr[-1] * ab_re - pi[-1] * ab_im)
        pi.append(pr[-2] * ab_im + pi[-1] * ab_re)
    p_re = jnp.stack(pr)
    p_im = jnp.stack(pi)
    cr = c_re.astype(F32)
    ci = c_im.astype(F32)
    cp_re = cr[None] * p_re[:, :, None, :] - ci[None] * p_im[:, :, None, :]
    cp_im = cr[None] * p_im[:, :, None, :] + ci[None] * p_re[:, :, None, :]
    kk = (jnp.einsum("jgon,gni->gjoi", cp_re[:t], bb_re, precision=hp)
          - jnp.einsum("jgon,gni->gjoi", cp_im[:t], bb_im, precision=hp))
    oc = LANES // c
    go = g // oc
    w = jnp.einsum("ogjci,gh->ojgihc", kk.reshape(go, oc, t, c, c), jnp.eye(oc, dtype=F32))
    w = w.reshape(go, t, LANES, LANES).astype(BF16)
    pw_re = p_re[:t][::-1]
    pw_im = p_im[:t][::-1]
    min_re = pw_re[:, :, :, None] * bb_re[None] - pw_im[:, :, :, None] * bb_im[None]
    min_im = pw_re[:, :, :, None] * bb_im[None] + pw_im[:, :, :, None] * bb_re[None]
    m_in = jnp.concatenate([jnp.transpose(min_re, (1, 0, 3, 2)), jnp.transpose(min_im, (1, 0, 3, 2))], axis=-1)
    m_in = jnp.transpose(m_in.reshape(go, oc, t, c, 2 * n), (0, 2, 1, 3, 4)).reshape(go, t * LANES, 2 * n)
    m_out = jnp.concatenate([cp_re[1:], -cp_im[1:]], axis=-1)
    m_out = jnp.transpose(m_out.reshape(t, go, oc, c, 2 * n), (1, 0, 2, 3, 4)).reshape(go, t * LANES, 2 * n)
    a_t_re, a_t_im = p_re[t], p_im[t]
    a1 = jnp.concatenate([a_t_re, a_t_re], axis=-1).reshape(1, g * 2 * n)
    a2 = jnp.concatenate([-a_t_im, a_t_im], axis=-1).reshape(1, g * 2 * n)
    return dict(w=w, m_in=m_in, m_out=m_out, a1=a1, a2=a2, d_row=d.astype(F32).reshape(1, g * c))


def _toeplitz_kernel(w_ref, o_ref, *, t):
    zero = jnp.zeros((LANES, LANES), o_ref.dtype)
    for s in range(t):
        for u in range(t):
            o_ref[0, s * LANES:(s + 1) * LANES, u * LANES:(u + 1) * LANES] = w_ref[0, u - s] if u >= s else zero


def _toeplitz(w):
    go, t = w.shape[:2]
    return pl.pallas_call(
        functools.partial(_toeplitz_kernel, t=t),
        grid=(go,),
        in_specs=[pl.BlockSpec((1, t, LANES, LANES), lambda o: (o, 0, 0, 0))],
        out_specs=pl.BlockSpec((1, t * LANES, t * LANES), lambda o: (o, 0, 0)),
        out_shape=jax.ShapeDtypeStruct((go, t * LANES, t * LANES), w.dtype),
        compiler_params=_cparams(("parallel",)),
    )(w)


def _expand(compact, c):
    oc = LANES // c
    tiled = jnp.concatenate([compact] * oc, axis=1)
    r = lax.broadcasted_iota(jnp.int32, tiled.shape, 0)
    q = lax.broadcasted_iota(jnp.int32, tiled.shape, 1)
    own = ((r & (LANES - 1)) >> int(math.log2(c))) == (q >> int(math.log2(LANES)))
    return jnp.where(own, tiled, 0.0)


def _gather_chunks(u_ref, xcat, t, rows):
    for s in range(t):
        xcat[:, s * LANES:(s + 1) * LANES] = u_ref[pl.ds(s, rows, stride=t), :]


def _s5_state_kernel(u_ref, min_ref, v_ref, xcat, *, t, rows, c):
    _gather_chunks(u_ref, xcat, t, rows)
    v_ref[...] = _dot3(xcat[...], _expand(min_ref[0], c))


def _s5_scan_kernel(v_ref, a1_ref, a2_ref, h0_ref, h_ref, f_ref, *, nchunk, half):
    a1 = a1_ref[...]
    a2 = a2_ref[...]
    lt = a1.shape[1]
    lane = lax.broadcasted_iota(jnp.int32, (1, lt), 1)
    is_re = (lane & (2 * half - 1)) < half

    def swap(h):
        return jnp.where(is_re, pltpu.roll(h, lt - half, axis=1), pltpu.roll(h, half, axis=1))

    def body(k, h):
        h_ref[pl.ds(k, 1), :] = h
        return a1 * h + a2 * swap(h) + v_ref[pl.ds(k, 1), :]

    f_ref[...] = lax.fori_loop(0, nchunk, body, h0_ref[...])


def _s5_out_kernel(u_ref, mi_ref, mo_ref, h_ref, d_ref, o_in, o_ref, xcat, *, t, rows, c):
    del o_in
    _gather_chunks(u_ref, xcat, t, rows)
    mo = _expand(mo_ref[0], c).astype(BF16)
    y = (jnp.dot(xcat[...].astype(BF16), mi_ref[0], preferred_element_type=F32)
         + lax.dot_general(h_ref[...].astype(BF16), mo, _NT, preferred_element_type=F32))
    d = d_ref[...]
    for s in range(t):
        sl = slice(s * LANES, (s + 1) * LANES)
        o_ref[pl.ds(s, rows, stride=t), :] = jax.nn.gelu(y[:, sl] + xcat[:, sl] * d)


def _s5(proj, hs, row0, bsz, seq, ucol, h0, ops, mi, groups):
    t = S5_T
    width = ops["d_row"].shape[1]
    c = width // groups
    gl = ops["a1"].shape[1]
    noct = width // LANES
    ocl = gl // noct
    nchunk = seq // t
    tokens = bsz * seq
    tb = _pick(tokens, (8192, 4096, 2048, 1024, 512, 256, 128))
    rb = tb // t
    assert row0 % tb == 0 and ucol % LANES == 0 and seq % t == 0
    u_spec = pl.BlockSpec((tb, LANES), lambda o, r: (row0 // tb + r, ucol // LANES + o))
    st_spec = pl.BlockSpec((rb, ocl), lambda o, r: (r, o))
    cmp_spec = pl.BlockSpec((1, t * LANES, LANES), lambda o, r: (o, 0, 0))
    v = pl.pallas_call(
        functools.partial(_s5_state_kernel, t=t, rows=rb, c=c),
        grid=(noct, tokens // tb),
        in_specs=[u_spec, cmp_spec],
        out_specs=st_spec,
        out_shape=jax.ShapeDtypeStruct((tokens // t, gl), F32),
        scratch_shapes=[pltpu.VMEM((rb, t * LANES), F32)],
        compiler_params=_cparams(("parallel", "parallel")),
    )(proj, ops["m_in"])

    lt = _pick(gl, (2048, 1024, 512, 256, 128))
    seq_blk = pl.BlockSpec((None, nchunk, lt), lambda b, j: (b, 0, j))
    vec_blk = pl.BlockSpec((1, lt), lambda b, j: (0, j))
    st_blk = pl.BlockSpec((None, 1, lt), lambda b, j: (b, 0, j))
    h, f = pl.pallas_call(
        functools.partial(_s5_scan_kernel, nchunk=nchunk, half=LANES // 2),
        grid=(bsz, gl // lt),
        in_specs=[seq_blk, vec_blk, vec_blk, st_blk],
        out_specs=[seq_blk, st_blk],
        out_shape=[jax.ShapeDtypeStruct((bsz, nchunk, gl), F32), jax.ShapeDtypeStruct((bsz, 1, gl), F32)],
        compiler_params=_cparams(("parallel", "parallel")),
    )(v.reshape(bsz, nchunk, gl), ops["a1"], ops["a2"], h0.reshape(bsz, 1, gl))

    hs = pl.pallas_call(
        functools.partial(_s5_out_kernel, t=t, rows=rb, c=c),
        grid=(noct, tokens // tb),
        in_specs=[u_spec,
                  pl.BlockSpec((1, t * LANES, t * LANES), lambda o, r: (o, 0, 0)),
                  cmp_spec, st_spec,
                  pl.BlockSpec((1, LANES), lambda o, r: (0, o)),
                  pl.BlockSpec(memory_space=pl.ANY)],
        out_specs=pl.BlockSpec((tb, LANES), lambda o, r: (row0 // tb + r, o)),
        out_shape=jax.ShapeDtypeStruct(hs.shape, hs.dtype),
        input_output_aliases={5: 0},
        scratch_shapes=[pltpu.VMEM((rb, t * LANES), F32)],
        compiler_params=_cparams(("parallel", "parallel")),
    )(proj, mi, ops["m_out"], h.reshape(tokens // t, gl), ops["d_row"], hs)
    return hs, f.reshape(bsz, gl)


def _interleave(re, im):
    return jnp.concatenate([re.astype(F32), im.astype(F32)], axis=-1).reshape(re.shape[0], -1)


def _deinterleave(h, groups):
    h = h.reshape(h.shape[0], groups, 2, -1)
    return h[:, :, 0], h[:, :, 1]


def _router_kernel(x_ref, w_ref, b_ref, o_ref, cnt_ref, carry):
    i = pl.program_id(0)

    @pl.when(i == 0)
    def _():
        carry[...] = jnp.zeros_like(carry)

    tm = x_ref.shape[0]
    logits = jnp.dot(x_ref[...].astype(BF16), w_ref[...], preferred_element_type=F32) + b_ref[...]
    lane = lax.broadcasted_iota(jnp.int32, logits.shape, 1)
    m1 = jnp.max(logits, axis=-1, keepdims=True)
    i1 = jnp.min(jnp.where(logits == m1, lane, LANES), axis=-1, keepdims=True)
    rest = jnp.where(lane == i1, -3e38, logits)
    m2 = jnp.max(rest, axis=-1, keepdims=True)
    i2 = jnp.min(jnp.where(rest == m2, lane, LANES), axis=-1, keepdims=True)
    e = jnp.exp(m2 - m1)
    w1 = 1.0 / (1.0 + e)
    w2 = e / (1.0 + e)
    onehot = jnp.where((lane == i1) | (lane == i2), 1.0, 0.0)
    r = lax.broadcasted_iota(jnp.int32, (tm, tm), 0)
    c = lax.broadcasted_iota(jnp.int32, (tm, tm), 1)
    tri = jnp.where(c < r, 1.0, 0.0).astype(BF16)
    rank = jnp.dot(tri, onehot.astype(BF16), preferred_element_type=F32) + carry[...]
    r1 = jnp.sum(jnp.where(lane == i1, rank, 0.0), axis=-1, keepdims=True)
    r2 = jnp.sum(jnp.where(lane == i2, rank, 0.0), axis=-1, keepdims=True)
    carry[...] += jnp.sum(onehot, axis=0, keepdims=True)
    cols = (i1.astype(F32), i2.astype(F32), w1, w2, r1, r2)
    out = jnp.zeros(logits.shape, F32)
    for idx, col in enumerate(cols):
        out = jnp.where(lane == idx, col, out)
    o_ref[...] = out
    cnt_ref[...] = carry[...]


def _router(xn, w_router, b_router):
    m, d = xn.shape
    e = w_router.shape[1]
    assert e <= LANES
    tm = _pick(m, (512, 256, 128, 64, 32, 16, 8))
    w = jnp.zeros((d, LANES), BF16).at[:, :e].set(w_router.astype(BF16))
    b = jnp.full((1, LANES), NEG, F32).at[0, :e].set(b_router.astype(F32))
    return pl.pallas_call(
        _router_kernel,
        grid=(m // tm,),
        in_specs=[pl.BlockSpec((tm, d), lambda i: (i, 0)), pl.BlockSpec((d, LANES), lambda i: (0, 0)),
                  pl.BlockSpec((1, LANES), lambda i: (0, 0))],
        out_specs=[pl.BlockSpec((tm, LANES), lambda i: (i, 0)), pl.BlockSpec((1, LANES), lambda i: (0, 0))],
        out_shape=[jax.ShapeDtypeStruct((m, LANES), F32), jax.ShapeDtypeStruct((1, LANES), F32)],
        scratch_shapes=[pltpu.VMEM((1, LANES), F32)],
        compiler_params=_cparams(("arbitrary",)),
    )(xn, w, b)


def _row_copy(src, dst, src_row, dst_row, sem):
    return pltpu.make_async_copy(src.at[pl.ds(src_row, 1)], dst.at[pl.ds(dst_row, 1)], sem)


def _dispatch_kernel(pos_ref, x_ref, zeros_hbm, xs_hbm, sem, *, tb):
    del zeros_hbm
    base = pl.program_id(0) * tb

    def issue(t, carry):
        for s in range(2):
            _row_copy(x_ref, xs_hbm, t, pos_ref[2 * (base + t) + s], sem).start()
        return carry

    lax.fori_loop(0, tb, issue, 0)

    def drain(t, carry):
        for s in range(2):
            _row_copy(x_ref, xs_hbm, 0, 0, sem).wait()
        return carry

    lax.fori_loop(0, tb, drain, 0)


def _dispatch(xn, pos, rows_padded):
    m, d = xn.shape
    tb = _pick(m, (512, 256, 128, 64, 32, 16, 8))
    grid_spec = pltpu.PrefetchScalarGridSpec(
        num_scalar_prefetch=1,
        grid=(m // tb,),
        in_specs=[pl.BlockSpec((tb, d), lambda i, pos: (i, 0)), pl.BlockSpec(memory_space=pl.ANY)],
        out_specs=pl.BlockSpec(memory_space=pl.ANY),
        scratch_shapes=[pltpu.SemaphoreType.DMA(())],
    )
    return pl.pallas_call(
        functools.partial(_dispatch_kernel, tb=tb),
        grid_spec=grid_spec,
        out_shape=jax.ShapeDtypeStruct((rows_padded, d), xn.dtype),
        input_output_aliases={2: 0},
        compiler_params=pltpu.CompilerParams(dimension_semantics=("arbitrary",), has_side_effects=True,
                                             vmem_limit_bytes=VMEM_LIMIT),
    )(pos, xn, jnp.zeros((rows_padded, d), xn.dtype))


def _combine_kernel(pos_ref, ys_hbm, x_ref, r_ref, o_ref, buf, sem, *, tb):
    base = pl.program_id(0) * tb

    def issue(t, carry):
        for s in range(2):
            _row_copy(ys_hbm, buf.at[s], pos_ref[2 * (base + t) + s], t, sem).start()
        return carry

    lax.fori_loop(0, tb, issue, 0)

    def drain(t, carry):
        for s in range(2):
            _row_copy(ys_hbm, buf.at[s], 0, 0, sem).wait()
        return carry

    lax.fori_loop(0, tb, drain, 0)
    r = r_ref[...]
    o_ref[...] = x_ref[...] + (r[:, 2:3] * buf[0] + r[:, 3:4] * buf[1])


def _combine(ys, pos, x, routing):
    m, d = x.shape
    tb = _pick(m, (256, 128, 64, 32, 16, 8))
    grid_spec = pltpu.PrefetchScalarGridSpec(
        num_scalar_prefetch=1,
        grid=(m // tb,),
        in_specs=[pl.BlockSpec(memory_space=pl.ANY),
                  pl.BlockSpec((tb, d), lambda i, pos: (i, 0)),
                  pl.BlockSpec((tb, LANES), lambda i, pos: (i, 0))],
        out_specs=pl.BlockSpec((tb, d), lambda i, pos: (i, 0)),
        scratch_shapes=[pltpu.VMEM((2, tb, d), F32), pltpu.SemaphoreType.DMA(())],
    )
    return pl.pallas_call(
        functools.partial(_combine_kernel, tb=tb),
        grid_spec=grid_spec,
        out_shape=jax.ShapeDtypeStruct((m, d), F32),
        compiler_params=_cparams(("arbitrary",)),
    )(pos, ys, x, routing)


def _moe(x, xn, w_router, b_router, wg, wu, wd):
    m, d = x.shape
    n_exp = wg.shape[0]
    tg = 512 if m >= 4096 else 32
    n_tiles = -(-(2 * m + n_exp * (tg - 1)) // tg)
    rows_padded = n_tiles * tg
    routing, counts = _router(xn, w_router, b_router)
    cnt = counts[0, :n_exp].astype(jnp.int32)
    padded = ((cnt + tg - 1) // tg) * tg
    ends = jnp.cumsum(padded)
    offs = ends - padded
    e1 = routing[:, 0].astype(jnp.int32)
    e2 = routing[:, 1].astype(jnp.int32)
    pos = jnp.stack([offs[e1] + routing[:, 4].astype(jnp.int32),
                     offs[e2] + routing[:, 5].astype(jnp.int32)], axis=1).reshape(-1)
    n_active = (ends[-1] // tg).reshape(1).astype(jnp.int32)
    tile_start = jnp.arange(n_tiles, dtype=jnp.int32) * tg
    tile_expert = jnp.minimum(jnp.sum(tile_start[:, None] >= ends[None, :], axis=1), n_exp - 1).astype(jnp.int32)
    xs = _dispatch(xn, pos, rows_padded)
    group = (tile_expert, n_active)
    f = wg.shape[-1]
    h = _matmul(xs, [wg, wu], _epi_swiglu, BF16, group=group, tm=tg,
                tn=_pick(f, (1408, 1024, 512, 256, 128)), tk=_pick(d, (2048, 1024, 512, 256, 128)))
    ys = _matmul(h, [wd], _epi_id, F32, group=group, tm=tg, tk=f)
    return _combine(ys, pos, x, routing)


def kernel(x_prompt, x_sample, cache_k, cache_v, state_ssm_re, state_ssm_im, rel_bias, norm_mix_g, w_in, b_gate, q_norm_g, k_norm_g, lambda_q1, lambda_k1, lambda_q2, lambda_k2, subln_g, w_attn_proj, ssm_a_re, ssm_a_im, ssm_log_dt, ssm_b_re, ssm_b_im, ssm_c_re, ssm_c_im, ssm_d, w_glu, b_glu, w_ssm_proj, w_out, norm_ffn_g, w_ff_gate, w_ff_up, w_ff_down, w_router, b_router, w_e_gate, w_e_up, w_e_down):
    bp, lp, d = x_prompt.shape
    bs, ls, _ = x_sample.shape
    depth = w_in.shape[0]
    past = cache_k.shape[2]
    n_heads, hd = cache_k.shape[3], cache_k.shape[5]
    qkw = n_heads * 2 * hd
    groups, n_state = state_ssm_re.shape[2], state_ssm_re.shape[3]
    ssm_w = ssm_d.shape[1]
    mp, ms = bp * lp, bs * ls
    assert hd % LANES == 0 and lp % S5_T == 0 and ls % S5_T == 0 and ssm_w % LANES == 0

    ck = cache_k.reshape(depth, bs, past, qkw)
    cv = cache_v.reshape(depth, bs, past, qkw)
    x = jnp.concatenate([x_prompt.reshape(mp, d), x_sample.reshape(ms, d)], axis=0)
    m = mp + ms

    d_ff = w_ff_gate.shape[-1]
    ff_pad = (-d_ff) % 1024 if d_ff > 1024 else 0
    outs = {k: [] for k in ("kp", "vp", "hrp", "hip", "ks", "vs", "hrs", "his")}
    zeros_state = jnp.zeros((bp, groups * 2 * n_state), F32)
    tables_p = _prompt_tables(rel_bias, lp)
    table_s = _sample_table(rel_bias, past, ls)

    for l in range(depth):
        lam_init = 0.8 - 0.6 * math.exp(-0.3 * l)
        xn = _rmsnorm(x, norm_mix_g[l], BF16)
        proj = _matmul(xn, [w_in[l].astype(BF16)], _epi_id, F32)
        qn, kf, kb, vf, vb = _qkv(proj, q_norm_g[l], k_norm_g[l], qkw, qkw, hd)
        lam4 = jnp.stack([lambda_q1[l], lambda_k1[l], lambda_q2[l], lambda_k2[l]]).astype(F32)
        g_sub = subln_g[l].reshape(1, 2 * hd).astype(F32)
        o = _attn_prompt(qn, kb, vb, bp, lp, tables_p, lam4, g_sub, n_heads, hd, lam_init)
        o = _attn_sample(qn, kb, vb, o, mp, bs, ls, ck, cv, l, table_s, lam4, g_sub, n_heads, hd, lam_init)
        attn_b = _matmul(o, [w_attn_proj[l].astype(BF16)], _epi_id, F32)

        ops = _s5_operators(ssm_a_re[l], ssm_a_im[l], ssm_log_dt[l], ssm_b_re[l], ssm_b_im[l],
                            ssm_c_re[l], ssm_c_im[l], ssm_d[l])
        mi = _toeplitz(ops["w"])
        hs = jnp.zeros((m, ssm_w), F32)
        hs, h_p = _s5(proj, hs, 0, bp, lp, 3 * qkw, zeros_state, ops, mi, groups)
        hs, h_s = _s5(proj, hs, mp, bs, ls, 3 * qkw, _interleave(state_ssm_re[l], state_ssm_im[l]), ops, mi, groups)
        hr_p, hi_p = _deinterleave(h_p, groups)
        hr_s, hi_s = _deinterleave(h_s, groups)
        hs = _matmul(hs, [w_glu[l].astype(BF16)], _epi_glu, BF16,
                     extras=[(hs, "tile", 0), (b_glu[l].reshape(1, -1).astype(F32), "row", 0)])
        tn = _pick(math.gcd(d, 3 * qkw + ssm_w), (1024, 512, 256, 128))
        goff = (3 * qkw + ssm_w) // tn
        bg = b_gate[l].reshape(1, -1).astype(F32)
        mix = _matmul(hs, [w_ssm_proj[l].astype(BF16)], _epi_mix, BF16, tn=tn,
                      extras=[(attn_b, "tile", 0), (proj, "tile", goff), (proj, "tile", goff + d // tn),
                              (bg, "row", 0), (bg, "row", d // tn)])
        x = _matmul(mix, [w_out[l].astype(BF16)], _epi_residual, F32, extras=[(x, "tile", 0)])

        i = l // 2
        if l % 2 == 0:
            xn = _rmsnorm(x, norm_ffn_g[l], BF16)
            wg = jnp.pad(w_ff_gate[i].astype(BF16), ((0, 0), (0, ff_pad)))
            wu = jnp.pad(w_ff_up[i].astype(BF16), ((0, 0), (0, ff_pad)))
            wd = jnp.pad(w_ff_down[i].astype(BF16), ((0, ff_pad), (0, 0)))
            h = _matmul(xn, [wg, wu], _epi_swiglu, BF16)
            x = _matmul(h, [wd], _epi_residual, F32, extras=[(x, "tile", 0)])
        else:
            xn = _rmsnorm(x, norm_ffn_g[l], F32)
            x = _moe(x, xn, w_router[i], b_router[i], w_e_gate[i].astype(BF16), w_e_up[i].astype(BF16),
                     w_e_down[i].astype(BF16))

        outs["kp"].append(kf[:mp].reshape(bp, lp, n_heads, 2, hd))
        outs["vp"].append(vf[:mp].reshape(bp, lp, n_heads, 2 * hd))
        outs["hrp"].append(hr_p)
        outs["hip"].append(hi_p)
        outs["ks"].append(kf[mp:].reshape(bs, ls, n_heads, 2, hd))
        outs["vs"].append(vf[mp:].reshape(bs, ls, n_heads, 2 * hd))
        outs["hrs"].append(hr_s)
        outs["his"].append(hi_s)

    st = {k: jnp.stack(v) for k, v in outs.items()}
    return (x[:mp].reshape(bp, lp, d), x[mp:].reshape(bs, ls, d), st["kp"], st["vp"], st["hrp"], st["hip"],
            st["ks"], st["vs"], st["hrs"], st["his"])
```

```python
import functools
import math

import jax
import jax.numpy as jnp
from jax import lax
from jax.experimental import pallas as pl
from jax.experimental.pallas import tpu as pltpu

F32 = jnp.float32
BF16 = jnp.bfloat16

CHUNK = 64
REL_MAX_DIST = 128
EPS = 1e-6
NEG = -1e30
LANES = 128
VMEM_LIMIT = 56 * 1024 * 1024
MM_VMEM_BUDGET = 50 * 1024 * 1024
S5_T = 16
LOG2E = math.log2(math.e)


def _pick(n, cands):
    for c in cands:
        if n % c == 0:
            return c
    return n


def _cparams(sem):
    return pltpu.CompilerParams(dimension_semantics=sem, vmem_limit_bytes=VMEM_LIMIT)


def _rmsnorm_kernel(x_ref, g_ref, o_ref):
    x = x_ref[...]
    y = x * lax.rsqrt(jnp.mean(x * x, axis=-1, keepdims=True) + EPS)
    o_ref[...] = (y * g_ref[...]).astype(o_ref.dtype)


def _rmsnorm(x, g, out_dtype):
    m, d = x.shape
    tm = _pick(m, (512, 256, 128, 64, 32, 16, 8))
    return pl.pallas_call(
        _rmsnorm_kernel,
        grid=(m // tm,),
        in_specs=[pl.BlockSpec((tm, d), lambda i: (i, 0)), pl.BlockSpec((1, d), lambda i: (0, 0))],
        out_specs=pl.BlockSpec((tm, d), lambda i: (i, 0)),
        out_shape=jax.ShapeDtypeStruct((m, d), out_dtype),
        compiler_params=_cparams(("parallel",)),
    )(x, g.reshape(1, d).astype(F32))


def _mm_kernel(*refs, nb, nx, nk, epi, grouped):
    if grouped:
        nact_ref = refs[1]
        refs = refs[2:]
    a_ref = refs[0]
    b_refs = refs[1:1 + nb]
    x_refs = refs[1 + nb:1 + nb + nx]
    o_ref = refs[1 + nb + nx]
    acc_refs = refs[2 + nb + nx:]

    def body():
        a = a_ref[...].astype(BF16)
        if nk == 1:
            accs = [jnp.dot(a, b[...], preferred_element_type=F32) for b in b_refs]
            o_ref[...] = epi(*accs, *[x[...] for x in x_refs]).astype(o_ref.dtype)
            return
        k = pl.program_id(2)

        @pl.when(k == 0)
        def _():
            for acc in acc_refs:
                acc[...] = jnp.zeros_like(acc)

        for acc, b in zip(acc_refs, b_refs):
            acc[...] += jnp.dot(a, b[...], preferred_element_type=F32)

        @pl.when(k == nk - 1)
        def _():
            o_ref[...] = epi(*[acc[...] for acc in acc_refs], *[x[...] for x in x_refs]).astype(o_ref.dtype)

    if grouped:
        active = pl.program_id(0) < nact_ref[0]
        pl.when(active)(body)

        @pl.when(jnp.logical_not(active))
        def _():
            o_ref[...] = jnp.zeros_like(o_ref)
    else:
        body()


def _mm_tiles(n, kdim, tm, tn, tk, nb, a_bytes, out_bytes, extras):
    def vmem(tn_, tk_):
        fixed = (2 * tm * tn_ * out_bytes + nb * tm * tn_ * 4
                 + sum(2 * (tm if kind == "tile" else 1) * tn_ * arr.dtype.itemsize for arr, kind, _ in extras))
        return fixed + tk_ * (2 * (tm * a_bytes + nb * tn_ * 2) + (tm * 2 if a_bytes > 2 else 0))

    tns = [tn] if tn else [c for c in (1024, 512, 256, 128) if n % c == 0] or [n]
    tks = [tk] if tk else [c for c in (kdim, 2816, 2048, 1024, 512, 256, 128) if kdim % c == 0]
    for tn_ in tns[:2]:
        if vmem(tn_, tks[0]) <= MM_VMEM_BUDGET:
            return tn_, tks[0]
    tn_ = tns[0] if nb == 1 else tns[min(1, len(tns) - 1)]
    return tn_, next((c for c in tks if vmem(tn_, c) <= MM_VMEM_BUDGET), tks[-1])


def _matmul(a, bs, epi, out_dtype, extras=(), group=None, tm=None, tn=None, tk=None):
    m, kdim = a.shape
    n = bs[0].shape[-1]
    tm = tm or _pick(m, (1024, 512, 256, 128, 64, 32, 16, 8))
    if tn is None or tk is None:
        tn, tk = _mm_tiles(n, kdim, tm, tn, tk, len(bs), a.dtype.itemsize, jnp.dtype(out_dtype).itemsize, extras)
    nk = kdim // tk
    grouped = group is not None

    if grouped:
        def row(i, te, na):
            return jnp.minimum(i, na[0] - 1)
        a_spec = pl.BlockSpec((tm, tk), lambda i, j, k, te, na: (row(i, te, na), k))
        b_specs = [pl.BlockSpec((None, tk, tn), lambda i, j, k, te, na: (te[row(i, te, na)], k, j)) for _ in bs]
        o_spec = pl.BlockSpec((tm, tn), lambda i, j, k, te, na: (i, j))
        x_specs = []
        assert not extras
    else:
        a_spec = pl.BlockSpec((tm, tk), lambda i, j, k: (i, k))
        b_specs = [pl.BlockSpec((tk, tn), lambda i, j, k: (k, j)) for _ in bs]
        o_spec = pl.BlockSpec((tm, tn), lambda i, j, k: (i, j))
        x_specs = []
        for arr, kind, off in extras:
            if kind == "tile":
                x_specs.append(pl.BlockSpec((tm, tn), lambda i, j, k, off=off: (i, j + off)))
            else:
                x_specs.append(pl.BlockSpec((1, tn), lambda i, j, k, off=off: (0, j + off)))

    scratch = [pltpu.VMEM((tm, tn), F32) for _ in bs] if nk > 1 else []
    kern = functools.partial(_mm_kernel, nb=len(bs), nx=len(extras), nk=nk, epi=epi, grouped=grouped)
    grid_spec = pltpu.PrefetchScalarGridSpec(
        num_scalar_prefetch=2 if grouped else 0,
        grid=(m // tm, n // tn, nk),
        in_specs=[a_spec, *b_specs, *x_specs],
        out_specs=o_spec,
        scratch_shapes=scratch,
    )
    args = ([group[0], group[1]] if grouped else []) + [a, *bs, *[e[0] for e in extras]]
    return pl.pallas_call(
        kern,
        grid_spec=grid_spec,
        out_shape=jax.ShapeDtypeStruct((m, n), out_dtype),
        compiler_params=_cparams(("arbitrary" if grouped else "parallel", "arbitrary" if grouped else "parallel",
                                  "arbitrary")),
    )(*args)


def _epi_id(acc):
    return acc


def _epi_residual(acc, x):
    return x + acc


def _epi_swiglu(g, u):
    return jax.nn.silu(g) * u


def _epi_glu(acc, hs, b):
    return hs.astype(F32) * jax.nn.sigmoid(acc + b)


def _epi_mix(ssm_b, attn_b, ga, gs, ba, bs):
    return jax.nn.sigmoid(ga + ba) * attn_b + jax.nn.sigmoid(gs + bs) * ssm_b


def _qkv_kernel(q_ref, k_ref, v_ref, gq_ref, gk_ref, qn_ref, kf_ref, kb_ref, vf_ref, vb_ref, *, hd):
    gq = gq_ref[...] * (hd ** -0.5 * LOG2E)
    gk = gk_ref[...]
    for j in range(q_ref.shape[1] // hd):
        sl = slice(j * hd, (j + 1) * hd)
        q = q_ref[:, sl]
        qn = q * lax.rsqrt(jnp.mean(q * q, axis=-1, keepdims=True) + EPS) * gq
        qn_ref[:, sl] = qn.astype(BF16)
        k = k_ref[:, sl]
        kn = k * lax.rsqrt(jnp.mean(k * k, axis=-1, keepdims=True) + EPS) * gk
        kf_ref[:, sl] = kn
        kb_ref[:, sl] = kn.astype(BF16)
    v = v_ref[...]
    vf_ref[...] = v
    vb_ref[...] = v.astype(BF16)


def _qkv(proj, gq, gk, qk_width, v_width, hd):
    m = proj.shape[0]
    assert qk_width == v_width
    w = qk_width
    tm = _pick(m, (256, 128, 64, 32, 16, 8))
    spec = lambda c: pl.BlockSpec((tm, w), lambda i, c=c: (i, c))
    ospec = pl.BlockSpec((tm, w), lambda i: (i, 0))
    gspec = pl.BlockSpec((1, hd), lambda i: (0, 0))
    return pl.pallas_call(
        functools.partial(_qkv_kernel, hd=hd),
        grid=(m // tm,),
        in_specs=[spec(0), spec(1), spec(2), gspec, gspec],
        out_specs=[ospec] * 5,
        out_shape=[jax.ShapeDtypeStruct((m, w), dt) for dt in (BF16, F32, BF16, F32, BF16)],
        compiler_params=_cparams(("parallel",)),
    )(proj, proj, proj, gq.reshape(1, hd).astype(F32), gk.reshape(1, hd).astype(F32))


def _rel_bucket(rel, n_buckets):
    half = n_buckets // 2
    max_exact = half // 2
    n = jnp.abs(rel)
    nf = jnp.maximum(n, 1).astype(F32)
    large = max_exact + (jnp.log(nf / max_exact) / math.log(REL_MAX_DIST / max_exact)
                         * (half - max_exact)).astype(jnp.int32)
    large = jnp.minimum(large, half - 1)
    return jnp.where(rel > 0, half, 0) + jnp.where(n < max_exact, n, large)


def _bias_table(rel_bias, q_pos, k_pos):
    rel = k_pos[None, :] - q_pos[:, None]
    bias = jnp.transpose(rel_bias.astype(F32)[_rel_bucket(rel, rel_bias.shape[0])], (2, 0, 1))
    visible = (k_pos[None, :] // CHUNK) <= (q_pos[:, None] // CHUNK)
    return jnp.where(visible[None], bias, NEG)


def _lambda(lam_ref, lam_init):
    lam = lam_ref[...]
    s1 = jnp.sum(lam[0:1] * lam[1:2], axis=-1, keepdims=True)
    s2 = jnp.sum(lam[2:3] * lam[3:4], axis=-1, keepdims=True)
    return jnp.exp(s1) - jnp.exp(s2) + lam_init


def _subln(o, g, lam_init):
    y = o * lax.rsqrt(jnp.mean(o * o, axis=-1, keepdims=True) + EPS)
    return (y * g) * (1.0 - lam_init)


_NT = (((1,), (1,)), ((), ()))


def _attn_prompt_kernel(qi_ref, kj_ref, ts_ref, q_ref, k_ref, v_ref, bias_ref, lam_ref, g_ref, o_ref,
                        m_sc, acc_sc, *, hd, lam_init):
    s = pl.program_id(2)
    qi = qi_ref[s]
    kj = kj_ref[s]

    @pl.when(kj == 0)
    def _():
        m_sc[...] = jnp.full_like(m_sc, NEG)
        acc_sc[...] = jnp.zeros_like(acc_sc)

    def step(with_bias):
        q = q_ref[...]
        k = k_ref[...]
        v = jnp.concatenate([v_ref[...], jnp.ones((k.shape[0], LANES), BF16)], axis=1)
        for c in range(2):
            sl = slice(c * hd, (c + 1) * hd)
            sc = lax.dot_general(q[:, sl], k[:, sl], _NT, preferred_element_type=F32)
            if with_bias:
                sc = sc + bias_ref[0, 0]
            m_prev = m_sc[c]
            m_new = jnp.maximum(m_prev, jnp.max(sc, axis=-1, keepdims=True))
            alpha = jnp.exp2(m_prev - m_new)
            p = jnp.exp2(sc - jnp.concatenate([m_new] * (sc.shape[1] // LANES), axis=1))
            acc_sc[c] = (jnp.concatenate([alpha] * (acc_sc.shape[2] // LANES), axis=1) * acc_sc[c]
                         + jnp.dot(p.astype(BF16), v, preferred_element_type=F32))
            m_sc[c] = m_new

    pl.when(ts_ref[s] < 2)(functools.partial(step, True))
    pl.when(ts_ref[s] == 2)(functools.partial(step, False))

    @pl.when(kj == qi)
    def _():
        lam = _lambda(lam_ref, lam_init)
        hw = 2 * hd
        o = []
        for c in range(2):
            acc = acc_sc[c]
            o.append(acc[:, :hw] / jnp.concatenate([acc[:, hw:]] * (hw // LANES), axis=1))
        o_ref[...] = _subln(o[0] - lam * o[1], g_ref[...], lam_init).astype(o_ref.dtype)


def _prompt_block(seq):
    blk = _pick(seq, (512, 256, 128))
    assert blk >= REL_MAX_DIST and blk % CHUNK == 0
    return blk


def _prompt_tables(rel_bias, seq):
    blk = _prompt_block(seq)
    pos = jnp.arange(blk, dtype=jnp.int32)
    far = _bias_table(rel_bias, pos[:1] + 2 * blk, pos[:1])
    near = jnp.stack([_bias_table(rel_bias, pos, pos), _bias_table(rel_bias, pos + blk, pos)])
    return jnp.where(near > 0.5 * NEG, (near - far[None]) * LOG2E, NEG)


def _attn_prompt(qn, kb, vb, bsz, seq, tables, lam4, g, n_heads, hd, lam_init):
    m, width = qn.shape
    hw = 2 * hd
    blk = _prompt_block(seq)
    nq = seq // blk
    pairs = [(i, j) for i in range(nq) for j in range(i + 1)]
    qi = jnp.array([p[0] for p in pairs], jnp.int32)
    kj = jnp.array([p[1] for p in pairs], jnp.int32)
    ts = jnp.array([min(p[0] - p[1], 2) for p in pairs], jnp.int32)
    grid_spec = pltpu.PrefetchScalarGridSpec(
        num_scalar_prefetch=3,
        grid=(bsz, n_heads, len(pairs)),
        in_specs=[
            pl.BlockSpec((blk, hw), lambda b, h, s, qi, kj, ts: (b * nq + qi[s], h)),
            pl.BlockSpec((blk, hw), lambda b, h, s, qi, kj, ts: (b * nq + kj[s], h)),
            pl.BlockSpec((blk, hw), lambda b, h, s, qi, kj, ts: (b * nq + kj[s], h)),
            pl.BlockSpec((1, 1, blk, blk), lambda b, h, s, qi, kj, ts: (jnp.minimum(ts[s], 1), h, 0, 0)),
            pl.BlockSpec((4, hd), lambda b, h, s, qi, kj, ts: (0, 0)),
            pl.BlockSpec((1, hw), lambda b, h, s, qi, kj, ts: (0, 0)),
        ],
        out_specs=pl.BlockSpec((blk, hw), lambda b, h, s, qi, kj, ts: (b * nq + qi[s], h)),
        scratch_shapes=[pltpu.VMEM((2, blk, LANES), F32),
                        pltpu.VMEM((2, blk, hw + LANES), F32)],
    )
    return pl.pallas_call(
        functools.partial(_attn_prompt_kernel, hd=hd, lam_init=lam_init),
        grid_spec=grid_spec,
        out_shape=jax.ShapeDtypeStruct((m, width), BF16),
        compiler_params=_cparams(("parallel", "parallel", "arbitrary")),
    )(qi, kj, ts, qn, kb, vb, tables, lam4, g)


def _attn_sample_kernel(q_ref, ck_ref, cv_ref, nk_ref, nv_ref, bp_ref, bn_ref, lam_ref, g_ref, o_in, o_ref,
                        *, hd, n_heads, past, lam_init):
    del o_in
    lam = _lambda(lam_ref, lam_init)
    hw = 2 * hd
    for h in range(n_heads):
        cols = slice(h * hw, (h + 1) * hw)
        q = q_ref[:, cols]
        nk = nk_ref[:, cols]
        nv = nv_ref[:, cols]
        cv = cv_ref[0, 0, :, cols].astype(BF16)
        bp = bp_ref[h]
        bn = bn_ref[h]
        probs = []
        for c in range(2):
            sl = slice(c * hd, (c + 1) * hd)
            ck = ck_ref[0, 0, pl.ds(2 * h + c, past, stride=2 * n_heads), :].astype(BF16)
            sp = lax.dot_general(q[:, sl], ck, _NT, preferred_element_type=F32) + bp
            sn = lax.dot_general(q[:, sl], nk[:, sl], _NT, preferred_element_type=F32) + bn
            m = jnp.maximum(jnp.max(sp, axis=-1, keepdims=True), jnp.max(sn, axis=-1, keepdims=True))
            pp = jnp.exp2(sp - m)
            pn = jnp.exp2(sn - m)
            inv = 1.0 / (jnp.sum(pp, axis=-1, keepdims=True) + jnp.sum(pn, axis=-1, keepdims=True))
            probs.append((pp * inv, pn * inv))
        wp = probs[0][0] - lam * probs[1][0]
        wn = probs[0][1] - lam * probs[1][1]
        o = (jnp.dot(wp.astype(BF16), cv, preferred_element_type=F32)
             + jnp.dot(wn.astype(BF16), nv, preferred_element_type=F32))
        o_ref[:, cols] = _subln(o, g_ref[...], lam_init).astype(o_ref.dtype)


def _sample_table(rel_bias, past, s_len):
    table = _bias_table(rel_bias, past + jnp.arange(s_len, dtype=jnp.int32),
                        jnp.arange(past + s_len, dtype=jnp.int32))
    return jnp.where(table > 0.5 * NEG, table * LOG2E, NEG)


def _attn_sample(qn, kb, vb, o, row0, bsz, s_len, cache_k, cache_v, layer, table, lam4, g, n_heads, hd, lam_init):
    width = qn.shape[1]
    hw = 2 * hd
    past = cache_v.shape[2]
    assert row0 % s_len == 0
    new_spec = pl.BlockSpec((s_len, width), lambda b: (row0 // s_len + b, 0))
    return pl.pallas_call(
        functools.partial(_attn_sample_kernel, hd=hd, n_heads=n_heads, past=past, lam_init=lam_init),
        grid=(bsz,),
        in_specs=[new_spec,
                  pl.BlockSpec((1, 1, past * n_heads * 2, hd), lambda b: (layer, b, 0, 0)),
                  pl.BlockSpec((1, 1, past, width), lambda b: (layer, b, 0, 0)),
                  new_spec, new_spec,
                  pl.BlockSpec((n_heads, s_len, past), lambda b: (0, 0, 0)),
                  pl.BlockSpec((n_heads, s_len, s_len), lambda b: (0, 0, 0)),
                  pl.BlockSpec((4, hd), lambda b: (0, 0)),
                  pl.BlockSpec((1, hw), lambda b: (0, 0)),
                  pl.BlockSpec(memory_space=pl.ANY)],
        out_specs=new_spec,
        out_shape=jax.ShapeDtypeStruct(o.shape, o.dtype),
        input_output_aliases={9: 0},
        compiler_params=_cparams(("parallel",)),
    )(qn, cache_k, cache_v, kb, vb, table[:, :, :past], table[:, :, past:], lam4, g, o)


def _split(x):
    hi = x.astype(BF16)
    lo = (x - hi.astype(F32)).astype(BF16)
    return hi, lo


def _dot3(a, b):
    ah, al = _split(a)
    bh, bl = _split(b)
    d = functools.partial(jnp.dot, preferred_element_type=F32)
    return d(ah, bh) + (d(ah, bl) + d(al, bh))


def _s5_operators(a_re, a_im, log_dt, b_re, b_im, c_re, c_im, d):
    hp = lax.Precision.HIGHEST
    g, n = a_re.shape
    c = b_re.shape[-1]
    t = S5_T
    assert 2 * n == LANES and LANES % c == 0
    dt = jnp.exp(log_dt.astype(F32))[:, None]
    lr = a_re.astype(F32)
    li = a_im.astype(F32)
    mag = jnp.exp(lr * dt)
    ab_re = mag * jnp.cos(li * dt)
    ab_im = mag * jnp.sin(li * dt)
    den = lr * lr + li * li
    num_re = ab_re - 1.0
    coef_re = (num_re * lr + ab_im * li) / den
    coef_im = (ab_im * lr - num_re * li) / den
    br = b_re.astype(F32)
    bi = b_im.astype(F32)
    bb_re = coef_re[..., None] * br - coef_im[..., None] * bi
    bb_im = coef_re[..., None] * bi + coef_im[..., None] * br
    pr = [jnp.ones_like(ab_re)]
    pi = [jnp.zeros_like(ab_re)]
    for _ in range(t):
        pr.append(pr[-1] * ab_re - pi[-1] * ab_im)
        pi.append(pr[-2] * ab_im + pi[-1] * ab_re)
    p_re = jnp.stack(pr)
    p_im = jnp.stack(pi)
    cr = c_re.astype(F32)
    ci = c_im.astype(F32)
    cp_re = cr[None] * p_re[:, :, None, :] - ci[None] * p_im[:, :, None, :]
    cp_im = cr[None] * p_im[:, :, None, :] + ci[None] * p_re[:, :, None, :]
    kk = (jnp.einsum("jgon,gni->gjoi", cp_re[:t], bb_re, precision=hp)
          - jnp.einsum("jgon,gni->gjoi", cp_im[:t], bb_im, precision=hp))
    oc = LANES // c
    go = g // oc
    w = jnp.einsum("ogjci,gh->ojgihc", kk.reshape(go, oc, t, c, c), jnp.eye(oc, dtype=F32))
    w = w.reshape(go, t, LANES, LANES).astype(BF16)
    pw_re = p_re[:t][::-1]
    pw_im = p_im[:t][::-1]
    min_re = pw_re[:, :, :, None] * bb_re[None] - pw_im[:, :, :, None] * bb_im[None]
    min_im = pw_re[:, :, :, None] * bb_im[None] + pw_im[:, :, :, None] * bb_re[None]
    m_in = jnp.concatenate([jnp.transpose(min_re, (1, 0, 3, 2)), jnp.transpose(min_im, (1, 0, 3, 2))], axis=-1)
    m_in = jnp.transpose(m_in.reshape(go, oc, t, c, 2 * n), (0, 2, 1, 3, 4)).reshape(go, t * LANES, 2 * n)
    m_out = jnp.concatenate([cp_re[1:], -cp_im[1:]], axis=-1)
    m_out = jnp.transpose(m_out.reshape(t, go, oc, c, 2 * n), (1, 0, 2, 3, 4)).reshape(go, t * LANES, 2 * n)
    a_t_re, a_t_im = p_re[t], p_im[t]
    a1 = jnp.concatenate([a_t_re, a_t_re], axis=-1).reshape(1, g * 2 * n)
    a2 = jnp.concatenate([-a_t_im, a_t_im], axis=-1).reshape(1, g * 2 * n)
    return dict(w=w, m_in=m_in, m_out=m_out, a1=a1, a2=a2, d_row=d.astype(F32).reshape(1, g * c))


def _toeplitz_kernel(w_ref, o_ref, *, t):
    zero = jnp.zeros((LANES, LANES), o_ref.dtype)
    for s in range(t):
        for u in range(t):
            o_ref[0, s * LANES:(s + 1) * LANES, u * LANES:(u + 1) * LANES] = w_ref[0, u - s] if u >= s else zero


def _toeplitz(w):
    go, t = w.shape[:2]
    return pl.pallas_call(
        functools.partial(_toeplitz_kernel, t=t),
        grid=(go,),
        in_specs=[pl.BlockSpec((1, t, LANES, LANES), lambda o: (o, 0, 0, 0))],
        out_specs=pl.BlockSpec((1, t * LANES, t * LANES), lambda o: (o, 0, 0)),
        out_shape=jax.ShapeDtypeStruct((go, t * LANES, t * LANES), w.dtype),
        compiler_params=_cparams(("parallel",)),
    )(w)


def _expand(compact, c):
    oc = LANES // c
    tiled = jnp.concatenate([compact] * oc, axis=1)
    r = lax.broadcasted_iota(jnp.int32, tiled.shape, 0)
    q = lax.broadcasted_iota(jnp.int32, tiled.shape, 1)
    own = ((r & (LANES - 1)) >> int(math.log2(c))) == (q >> int(math.log2(LANES)))
    return jnp.where(own, tiled, 0.0)


def _gather_chunks(u_ref, xcat, t, rows):
    for s in range(t):
        xcat[:, s * LANES:(s + 1) * LANES] = u_ref[pl.ds(s, rows, stride=t), :]


def _s5_state_kernel(u_ref, min_ref, v_ref, xcat, *, t, rows, c):
    _gather_chunks(u_ref, xcat, t, rows)
    v_ref[...] = _dot3(xcat[...], _expand(min_ref[0], c))


def _s5_scan_kernel(v_ref, a1_ref, a2_ref, h0_ref, h_ref, f_ref, vs_sc, *, nchunk, half):
    a1 = a1_ref[...]
    a2 = a2_ref[...]
    lt = a1.shape[1]

    def swap(x):
        lane = lax.broadcasted_iota(jnp.int32, x.shape, 1)
        is_re = (lane & (2 * half - 1)) < half
        return jnp.where(is_re, pltpu.roll(x, lt - half, axis=1), pltpu.roll(x, half, axis=1))

    vs_sc[...] = swap(v_ref[...])
    a2s = swap(a2)
    h0 = h0_ref[...]

    def body(k, carry):
        h, g = carry
        h_ref[pl.ds(k, 1), :] = h
        return (a1 * h + a2 * g + v_ref[pl.ds(k, 1), :], a1 * g + a2s * h + vs_sc[pl.ds(k, 1), :])

    h, _ = lax.fori_loop(0, nchunk, body, (h0, swap(h0)))
    f_ref[...] = h


def _s5_out_kernel(u_ref, mi_ref, mo_ref, h_ref, d_ref, o_in, o_ref, xcat, *, t, rows, c):
    del o_in
    _gather_chunks(u_ref, xcat, t, rows)
    mo = _expand(mo_ref[0], c).astype(BF16)
    y = (jnp.dot(xcat[...].astype(BF16), mi_ref[0], preferred_element_type=F32)
         + lax.dot_general(h_ref[...].astype(BF16), mo, _NT, preferred_element_type=F32))
    d = d_ref[...]
    for s in range(t):
        sl = slice(s * LANES, (s + 1) * LANES)
        o_ref[pl.ds(s, rows, stride=t), :] = jax.nn.gelu(y[:, sl] + xcat[:, sl] * d)


def _s5(proj, hs, row0, bsz, seq, ucol, h0, ops, mi, groups):
    t = S5_T
    width = ops["d_row"].shape[1]
    c = width // groups
    gl = ops["a1"].shape[1]
    noct = width // LANES
    ocl = gl // noct
    nchunk = seq // t
    tokens = bsz * seq
    tb = _pick(tokens, (8192, 4096, 2048, 1024, 512, 256, 128))
    rb = tb // t
    assert row0 % tb == 0 and ucol % LANES == 0 and seq % t == 0
    u_spec = pl.BlockSpec((tb, LANES), lambda o, r: (row0 // tb + r, ucol // LANES + o))
    st_spec = pl.BlockSpec((rb, ocl), lambda o, r: (r, o))
    cmp_spec = pl.BlockSpec((1, t * LANES, LANES), lambda o, r: (o, 0, 0))
    v = pl.pallas_call(
        functools.partial(_s5_state_kernel, t=t, rows=rb, c=c),
        grid=(noct, tokens // tb),
        in_specs=[u_spec, cmp_spec],
        out_specs=st_spec,
        out_shape=jax.ShapeDtypeStruct((tokens // t, gl), F32),
        scratch_shapes=[pltpu.VMEM((rb, t * LANES), F32)],
        compiler_params=_cparams(("parallel", "parallel")),
    )(proj, ops["m_in"])

    lt = _pick(gl, (2048, 1024, 512, 256, 128))
    seq_blk = pl.BlockSpec((None, nchunk, lt), lambda b, j: (b, 0, j))
    vec_blk = pl.BlockSpec((1, lt), lambda b, j: (0, j))
    st_blk = pl.BlockSpec((None, 1, lt), lambda b, j: (b, 0, j))
    h, f = pl.pallas_call(
        functools.partial(_s5_scan_kernel, nchunk=nchunk, half=LANES // 2),
        grid=(bsz, gl // lt),
        in_specs=[seq_blk, vec_blk, vec_blk, st_blk],
        out_specs=[seq_blk, st_blk],
        out_shape=[jax.ShapeDtypeStruct((bsz, nchunk, gl), F32), jax.ShapeDtypeStruct((bsz, 1, gl), F32)],
        scratch_shapes=[pltpu.VMEM((nchunk, lt), F32)],
        compiler_params=_cparams(("parallel", "parallel")),
    )(v.reshape(bsz, nchunk, gl), ops["a1"], ops["a2"], h0.reshape(bsz, 1, gl))

    hs = pl.pallas_call(
        functools.partial(_s5_out_kernel, t=t, rows=rb, c=c),
        grid=(noct, tokens // tb),
        in_specs=[u_spec,
                  pl.BlockSpec((1, t * LANES, t * LANES), lambda o, r: (o, 0, 0)),
                  cmp_spec, st_spec,
                  pl.BlockSpec((1, LANES), lambda o, r: (0, o)),
                  pl.BlockSpec(memory_space=pl.ANY)],
        out_specs=pl.BlockSpec((tb, LANES), lambda o, r: (row0 // tb + r, o)),
        out_shape=jax.ShapeDtypeStruct(hs.shape, hs.dtype),
        input_output_aliases={5: 0},
        scratch_shapes=[pltpu.VMEM((rb, t * LANES), F32)],
        compiler_params=_cparams(("parallel", "parallel")),
    )(proj, mi, ops["m_out"], h.reshape(tokens // t, gl), ops["d_row"], hs)
    return hs, f.reshape(bsz, gl)


def _interleave(re, im):
    return jnp.concatenate([re.astype(F32), im.astype(F32)], axis=-1).reshape(re.shape[0], -1)


def _deinterleave(h, groups):
    h = h.reshape(h.shape[0], groups, 2, -1)
    return h[:, :, 0], h[:, :, 1]


def _router_kernel(x_ref, w_ref, b_ref, o_ref, cnt_ref, carry):
    i = pl.program_id(0)

    @pl.when(i == 0)
    def _():
        carry[...] = jnp.zeros_like(carry)

    tm = x_ref.shape[0]
    logits = jnp.dot(x_ref[...].astype(BF16), w_ref[...], preferred_element_type=F32) + b_ref[...]
    lane = lax.broadcasted_iota(jnp.int32, logits.shape, 1)
    m1 = jnp.max(logits, axis=-1, keepdims=True)
    i1 = jnp.min(jnp.where(logits == m1, lane, LANES), axis=-1, keepdims=True)
    rest = jnp.where(lane == i1, -3e38, logits)
    m2 = jnp.max(rest, axis=-1, keepdims=True)
    i2 = jnp.min(jnp.where(rest == m2, lane, LANES), axis=-1, keepdims=True)
    e = jnp.exp(m2 - m1)
    w1 = 1.0 / (1.0 + e)
    w2 = e / (1.0 + e)
    onehot = jnp.where((lane == i1) | (lane == i2), 1.0, 0.0)
    r = lax.broadcasted_iota(jnp.int32, (tm, tm), 0)
    c = lax.broadcasted_iota(jnp.int32, (tm, tm), 1)
    tri = jnp.where(c < r, 1.0, 0.0).astype(BF16)
    rank = jnp.dot(tri, onehot.astype(BF16), preferred_element_type=F32) + carry[...]
    r1 = jnp.sum(jnp.where(lane == i1, rank, 0.0), axis=-1, keepdims=True)
    r2 = jnp.sum(jnp.where(lane == i2, rank, 0.0), axis=-1, keepdims=True)
    carry[...] += jnp.sum(onehot, axis=0, keepdims=True)
    cols = (i1.astype(F32), i2.astype(F32), w1, w2, r1, r2)
    out = jnp.zeros(logits.shape, F32)
    for idx, col in enumerate(cols):
        out = jnp.where(lane == idx, col, out)
    o_ref[...] = out
    cnt_ref[...] = carry[...]


def _router(xn, w_router, b_router):
    m, d = xn.shape
    e = w_router.shape[1]
    assert e <= LANES
    tm = _pick(m, (512, 256, 128, 64, 32, 16, 8))
    w = jnp.zeros((d, LANES), BF16).at[:, :e].set(w_router.astype(BF16))
    b = jnp.full((1, LANES), NEG, F32).at[0, :e].set(b_router.astype(F32))
    return pl.pallas_call(
        _router_kernel,
        grid=(m // tm,),
        in_specs=[pl.BlockSpec((tm, d), lambda i: (i, 0)), pl.BlockSpec((d, LANES), lambda i: (0, 0)),
                  pl.BlockSpec((1, LANES), lambda i: (0, 0))],
        out_specs=[pl.BlockSpec((tm, LANES), lambda i: (i, 0)), pl.BlockSpec((1, LANES), lambda i: (0, 0))],
        out_shape=[jax.ShapeDtypeStruct((m, LANES), F32), jax.ShapeDtypeStruct((1, LANES), F32)],
        scratch_shapes=[pltpu.VMEM((1, LANES), F32)],
        compiler_params=_cparams(("arbitrary",)),
    )(xn, w, b)


def _row_copy(src, dst, src_row, dst_row, sem):
    return pltpu.make_async_copy(src.at[pl.ds(src_row, 1)], dst.at[pl.ds(dst_row, 1)], sem)


def _dispatch_kernel(pos_ref, x_ref, zeros_hbm, xs_hbm, sem, *, tb):
    del zeros_hbm
    base = pl.program_id(0) * tb

    def issue(t, carry):
        for s in range(2):
            _row_copy(x_ref, xs_hbm, t, pos_ref[2 * (base + t) + s], sem).start()
        return carry

    lax.fori_loop(0, tb, issue, 0)

    def drain(t, carry):
        for s in range(2):
            _row_copy(x_ref, xs_hbm, 0, 0, sem).wait()
        return carry

    lax.fori_loop(0, tb, drain, 0)


def _dispatch(xn, pos, rows_padded):
    m, d = xn.shape
    tb = _pick(m, (512, 256, 128, 64, 32, 16, 8))
    grid_spec = pltpu.PrefetchScalarGridSpec(
        num_scalar_prefetch=1,
        grid=(m // tb,),
        in_specs=[pl.BlockSpec((tb, d), lambda i, pos: (i, 0)), pl.BlockSpec(memory_space=pl.ANY)],
        out_specs=pl.BlockSpec(memory_space=pl.ANY),
        scratch_shapes=[pltpu.SemaphoreType.DMA(())],
    )
    return pl.pallas_call(
        functools.partial(_dispatch_kernel, tb=tb),
        grid_spec=grid_spec,
        out_shape=jax.ShapeDtypeStruct((rows_padded, d), xn.dtype),
        input_output_aliases={2: 0},
        compiler_params=pltpu.CompilerParams(dimension_semantics=("arbitrary",), has_side_effects=True,
                                             vmem_limit_bytes=VMEM_LIMIT),
    )(pos, xn, jnp.zeros((rows_padded, d), xn.dtype))


def _combine_kernel(pos_ref, ys_hbm, x_ref, r_ref, o_ref, buf, sem, *, tb):
    base = pl.program_id(0) * tb

    def issue(t, carry):
        for s in range(2):
            _row_copy(ys_hbm, buf.at[s], pos_ref[2 * (base + t) + s], t, sem).start()
        return carry

    lax.fori_loop(0, tb, issue, 0)

    def drain(t, carry):
        for s in range(2):
            _row_copy(ys_hbm, buf.at[s], 0, 0, sem).wait()
        return carry

    lax.fori_loop(0, tb, drain, 0)
    r = r_ref[...]
    o_ref[...] = x_ref[...] + (r[:, 2:3] * buf[0] + r[:, 3:4] * buf[1])


def _combine(ys, pos, x, routing):
    m, d = x.shape
    tb = _pick(m, (256, 128, 64, 32, 16, 8))
    grid_spec = pltpu.PrefetchScalarGridSpec(
        num_scalar_prefetch=1,
        grid=(m // tb,),
        in_specs=[pl.BlockSpec(memory_space=pl.ANY),
                  pl.BlockSpec((tb, d), lambda i, pos: (i, 0)),
                  pl.BlockSpec((tb, LANES), lambda i, pos: (i, 0))],
        out_specs=pl.BlockSpec((tb, d), lambda i, pos: (i, 0)),
        scratch_shapes=[pltpu.VMEM((2, tb, d), F32), pltpu.SemaphoreType.DMA(())],
    )
    return pl.pallas_call(
        functools.partial(_combine_kernel, tb=tb),
        grid_spec=grid_spec,
        out_shape=jax.ShapeDtypeStruct((m, d), F32),
        compiler_params=_cparams(("arbitrary",)),
    )(pos, ys, x, routing)


def _moe(x, xn, w_router, b_router, wg, wu, wd):
    m, d = x.shape
    n_exp = wg.shape[0]
    tg = 512 if m >= 4096 else 32
    n_tiles = -(-(2 * m + n_exp * (tg - 1)) // tg)
    rows_padded = n_tiles * tg
    routing, counts = _router(xn, w_router, b_router)
    cnt = counts[0, :n_exp].astype(jnp.int32)
    padded = ((cnt + tg - 1) // tg) * tg
    ends = jnp.cumsum(padded)
    offs = ends - padded
    lanes = jnp.arange(LANES, dtype=jnp.int32)
    offs_row = jnp.zeros((LANES,), jnp.int32).at[:n_exp].set(offs)

    def row_of(col_expert, col_rank):
        e = routing[:, col_expert].astype(jnp.int32)
        off = jnp.sum(jnp.where(e[:, None] == lanes[None, :], offs_row[None, :], 0), axis=1)
        return off + routing[:, col_rank].astype(jnp.int32)

    pos = jnp.stack([row_of(0, 4), row_of(1, 5)], axis=1).reshape(-1)
    n_active = (ends[-1] // tg).reshape(1).astype(jnp.int32)
    tile_start = jnp.arange(n_tiles, dtype=jnp.int32) * tg
    tile_expert = jnp.minimum(jnp.sum(tile_start[:, None] >= ends[None, :], axis=1), n_exp - 1).astype(jnp.int32)
    xs = _dispatch(xn, pos, rows_padded)
    group = (tile_expert, n_active)
    f = wg.shape[-1]
    h = _matmul(xs, [wg, wu], _epi_swiglu, BF16, group=group, tm=tg,
                tn=_pick(f, (1408, 1024, 512, 256, 128)), tk=_pick(d, (2048, 1024, 512, 256, 128)))
    ys = _matmul(h, [wd], _epi_id, F32, group=group, tm=tg, tk=f)
    return _combine(ys, pos, x, routing)


def kernel(x_prompt, x_sample, cache_k, cache_v, state_ssm_re, state_ssm_im, rel_bias, norm_mix_g, w_in, b_gate, q_norm_g, k_norm_g, lambda_q1, lambda_k1, lambda_q2, lambda_k2, subln_g, w_attn_proj, ssm_a_re, ssm_a_im, ssm_log_dt, ssm_b_re, ssm_b_im, ssm_c_re, ssm_c_im, ssm_d, w_glu, b_glu, w_ssm_proj, w_out, norm_ffn_g, w_ff_gate, w_ff_up, w_ff_down, w_router, b_router, w_e_gate, w_e_up, w_e_down):
    bp, lp, d = x_prompt.shape
    bs, ls, _ = x_sample.shape
    depth = w_in.shape[0]
    past = cache_k.shape[2]
    n_heads, hd = cache_k.shape[3], cache_k.shape[5]
    qkw = n_heads * 2 * hd
    groups, n_state = state_ssm_re.shape[2], state_ssm_re.shape[3]
    ssm_w = ssm_d.shape[1]
    mp, ms = bp * lp, bs * ls
    assert hd % LANES == 0 and lp % S5_T == 0 and ls % S5_T == 0 and ssm_w % LANES == 0

    ck = cache_k.reshape(depth, bs, past * n_heads * 2, hd)
    cv = cache_v.reshape(depth, bs, past, qkw)
    x = jnp.concatenate([x_prompt.reshape(mp, d), x_sample.reshape(ms, d)], axis=0)
    m = mp + ms

    d_ff = w_ff_gate.shape[-1]
    ff_pad = (-d_ff) % 1024 if d_ff > 1024 else 0
    outs = {k: [] for k in ("kp", "vp", "hrp", "hip", "ks", "vs", "hrs", "his")}
    zeros_state = jnp.zeros((bp, groups * 2 * n_state), F32)
    tables_p = _prompt_tables(rel_bias, lp)
    table_s = _sample_table(rel_bias, past, ls)

    for l in range(depth):
        lam_init = 0.8 - 0.6 * math.exp(-0.3 * l)
        xn = _rmsnorm(x, norm_mix_g[l], BF16)
        proj = _matmul(xn, [w_in[l].astype(BF16)], _epi_id, F32)
        qn, kf, kb, vf, vb = _qkv(proj, q_norm_g[l], k_norm_g[l], qkw, qkw, hd)
        lam4 = jnp.stack([lambda_q1[l], lambda_k1[l], lambda_q2[l], lambda_k2[l]]).astype(F32)
        g_sub = subln_g[l].reshape(1, 2 * hd).astype(F32)
        o = _attn_prompt(qn, kb, vb, bp, lp, tables_p, lam4, g_sub, n_heads, hd, lam_init)
        o = _attn_sample(qn, kb, vb, o, mp, bs, ls, ck, cv, l, table_s, lam4, g_sub, n_heads, hd, lam_init)
        attn_b = _matmul(o, [w_attn_proj[l].astype(BF16)], _epi_id, F32)

        ops = _s5_operators(ssm_a_re[l], ssm_a_im[l], ssm_log_dt[l], ssm_b_re[l], ssm_b_im[l],
                            ssm_c_re[l], ssm_c_im[l], ssm_d[l])
        mi = _toeplitz(ops["w"])
        hs = jnp.zeros((m, ssm_w), F32)
        hs, h_p = _s5(proj, hs, 0, bp, lp, 3 * qkw, zeros_state, ops, mi, groups)
        hs, h_s = _s5(proj, hs, mp, bs, ls, 3 * qkw, _interleave(state_ssm_re[l], state_ssm_im[l]), ops, mi, groups)
        hr_p, hi_p = _deinterleave(h_p, groups)
        hr_s, hi_s = _deinterleave(h_s, groups)
        hs = _matmul(hs, [w_glu[l].astype(BF16)], _epi_glu, BF16,
                     extras=[(hs, "tile", 0), (b_glu[l].reshape(1, -1).astype(F32), "row", 0)])
        tn = _pick(math.gcd(d, 3 * qkw + ssm_w), (1024, 512, 256, 128))
        goff = (3 * qkw + ssm_w) // tn
        bg = b_gate[l].reshape(1, -1).astype(F32)
        mix = _matmul(hs, [w_ssm_proj[l].astype(BF16)], _epi_mix, BF16, tn=tn,
                      extras=[(attn_b, "tile", 0), (proj, "tile", goff), (proj, "tile", goff + d // tn),
                              (bg, "row", 0), (bg, "row", d // tn)])
        x = _matmul(mix, [w_out[l].astype(BF16)], _epi_residual, F32, extras=[(x, "tile", 0)])

        i = l // 2
        if l % 2 == 0:
            xn = _rmsnorm(x, norm_ffn_g[l], BF16)
            wg = jnp.pad(w_ff_gate[i].astype(BF16), ((0, 0), (0, ff_pad)))
            wu = jnp.pad(w_ff_up[i].astype(BF16), ((0, 0), (0, ff_pad)))
            wd = jnp.pad(w_ff_down[i].astype(BF16), ((0, ff_pad), (0, 0)))
            h = _matmul(xn, [wg, wu], _epi_swiglu, BF16)
            x = _matmul(h, [wd], _epi_residual, F32, extras=[(x, "tile", 0)])
        else:
            xn = _rmsnorm(x, norm_ffn_g[l], F32)
            x = _moe(x, xn, w_router[i], b_router[i], w_e_gate[i].astype(BF16), w_e_up[i].astype(BF16),
                     w_e_down[i].astype(BF16))

        outs["kp"].append(kf[:mp].reshape(bp, lp, n_heads, 2, hd))
        outs["vp"].append(vf[:mp].reshape(bp, lp, n_heads, 2 * hd))
        outs["hrp"].append(hr_p)
        outs["hip"].append(hi_p)
        outs["ks"].append(kf[mp:].reshape(bs, ls, n_heads, 2, hd))
        outs["vs"].append(vf[mp:].reshape(bs, ls, n_heads, 2 * hd))
        outs["hrs"].append(hr_s)
        outs["his"].append(hi_s)

    st = {k: jnp.stack(v) for k, v in outs.items()}
    return (x[:mp].reshape(bp, lp, d), x[mp:].reshape(bs, ls, d), st["kp"], st["vp"], st["hrp"], st["hip"],
            st["ks"], st["vs"], st["hrs"], st["his"])
```

```python
import functools
import math

import jax
import jax.numpy as jnp
from jax import lax
from jax.experimental import pallas as pl
from jax.experimental.pallas import tpu as pltpu

F32 = jnp.float32
BF16 = jnp.bfloat16

CHUNK = 64
REL_MAX_DIST = 128
EPS = 1e-6
NEG = -1e30
LANES = 128
VMEM_LIMIT = 56 * 1024 * 1024
MM_VMEM_BUDGET = 50 * 1024 * 1024
S5_T = 16
LOG2E = math.log2(math.e)


def _pick(n, cands):
    for c in cands:
        if n % c == 0:
            return c
    return n


def _cparams(sem):
    return pltpu.CompilerParams(dimension_semantics=sem, vmem_limit_bytes=VMEM_LIMIT)


def _rmsnorm_kernel(x_ref, g_ref, o_ref):
    x = x_ref[...]
    y = x * lax.rsqrt(jnp.mean(x * x, axis=-1, keepdims=True) + EPS)
    o_ref[...] = (y * g_ref[...]).astype(o_ref.dtype)


def _rmsnorm(x, g, out_dtype):
    m, d = x.shape
    tm = _pick(m, (512, 256, 128, 64, 32, 16, 8))
    return pl.pallas_call(
        _rmsnorm_kernel,
        grid=(m // tm,),
        in_specs=[pl.BlockSpec((tm, d), lambda i: (i, 0)), pl.BlockSpec((1, d), lambda i: (0, 0))],
        out_specs=pl.BlockSpec((tm, d), lambda i: (i, 0)),
        out_shape=jax.ShapeDtypeStruct((m, d), out_dtype),
        compiler_params=_cparams(("parallel",)),
    )(x, g.reshape(1, d).astype(F32))


def _mm_kernel(*refs, nb, nx, nk, epi, grouped):
    if grouped:
        nact_ref = refs[1]
        refs = refs[2:]
    a_ref = refs[0]
    b_refs = refs[1:1 + nb]
    x_refs = refs[1 + nb:1 + nb + nx]
    o_ref = refs[1 + nb + nx]
    acc_refs = refs[2 + nb + nx:]

    def body():
        a = a_ref[...].astype(BF16)
        if nk == 1:
            accs = [jnp.dot(a, b[...], preferred_element_type=F32) for b in b_refs]
            o_ref[...] = epi(*accs, *[x[...] for x in x_refs]).astype(o_ref.dtype)
            return
        k = pl.program_id(2)

        @pl.when(k == 0)
        def _():
            for acc in acc_refs:
                acc[...] = jnp.zeros_like(acc)

        for acc, b in zip(acc_refs, b_refs):
            acc[...] += jnp.dot(a, b[...], preferred_element_type=F32)

        @pl.when(k == nk - 1)
        def _():
            o_ref[...] = epi(*[acc[...] for acc in acc_refs], *[x[...] for x in x_refs]).astype(o_ref.dtype)

    if grouped:
        active = pl.program_id(0) < nact_ref[0]
        pl.when(active)(body)

        @pl.when(jnp.logical_not(active))
        def _():
            o_ref[...] = jnp.zeros_like(o_ref)
    else:
        body()


def _mm_tiles(n, kdim, tm, tn, tk, nb, a_bytes, out_bytes, extras):
    def vmem(tn_, tk_):
        fixed = (2 * tm * tn_ * out_bytes + nb * tm * tn_ * 4
                 + sum(2 * (tm if kind == "tile" else 1) * tn_ * arr.dtype.itemsize for arr, kind, _ in extras))
        return fixed + tk_ * (2 * (tm * a_bytes + nb * tn_ * 2) + (tm * 2 if a_bytes > 2 else 0))

    tns = [tn] if tn else [c for c in (1024, 512, 256, 128) if n % c == 0] or [n]
    tks = [tk] if tk else [c for c in (kdim, 2816, 2048, 1024, 512, 256, 128) if kdim % c == 0]
    for tn_ in tns[:2]:
        if vmem(tn_, tks[0]) <= MM_VMEM_BUDGET:
            return tn_, tks[0]
    tn_ = tns[0] if nb == 1 else tns[min(1, len(tns) - 1)]
    return tn_, next((c for c in tks if vmem(tn_, c) <= MM_VMEM_BUDGET), tks[-1])


def _matmul(a, bs, epi, out_dtype, extras=(), group=None, tm=None, tn=None, tk=None):
    m, kdim = a.shape
    n = bs[0].shape[-1]
    tm = tm or _pick(m, (1024, 512, 256, 128, 64, 32, 16, 8))
    if tn is None or tk is None:
        tn, tk = _mm_tiles(n, kdim, tm, tn, tk, len(bs), a.dtype.itemsize, jnp.dtype(out_dtype).itemsize, extras)
    nk = kdim // tk
    grouped = group is not None

    if grouped:
        def row(i, te, na):
            return jnp.minimum(i, na[0] - 1)
        a_spec = pl.BlockSpec((tm, tk), lambda i, j, k, te, na: (row(i, te, na), k))
        b_specs = [pl.BlockSpec((None, tk, tn), lambda i, j, k, te, na: (te[row(i, te, na)], k, j)) for _ in bs]
        o_spec = pl.BlockSpec((tm, tn), lambda i, j, k, te, na: (i, j))
        x_specs = []
        assert not extras
    else:
        a_spec = pl.BlockSpec((tm, tk), lambda i, j, k: (i, k))
        b_specs = [pl.BlockSpec((tk, tn), lambda i, j, k: (k, j)) for _ in bs]
        o_spec = pl.BlockSpec((tm, tn), lambda i, j, k: (i, j))
        x_specs = []
        for arr, kind, off in extras:
            if kind == "tile":
                x_specs.append(pl.BlockSpec((tm, tn), lambda i, j, k, off=off: (i, j + off)))
            else:
                x_specs.append(pl.BlockSpec((1, tn), lambda i, j, k, off=off: (0, j + off)))

    scratch = [pltpu.VMEM((tm, tn), F32) for _ in bs] if nk > 1 else []
    kern = functools.partial(_mm_kernel, nb=len(bs), nx=len(extras), nk=nk, epi=epi, grouped=grouped)
    grid_spec = pltpu.PrefetchScalarGridSpec(
        num_scalar_prefetch=2 if grouped else 0,
        grid=(m // tm, n // tn, nk),
        in_specs=[a_spec, *b_specs, *x_specs],
        out_specs=o_spec,
        scratch_shapes=scratch,
    )
    args = ([group[0], group[1]] if grouped else []) + [a, *bs, *[e[0] for e in extras]]
    return pl.pallas_call(
        kern,
        grid_spec=grid_spec,
        out_shape=jax.ShapeDtypeStruct((m, n), out_dtype),
        compiler_params=_cparams(("arbitrary" if grouped else "parallel", "arbitrary" if grouped else "parallel",
                                  "arbitrary")),
    )(*args)


def _epi_id(acc):
    return acc


def _epi_residual(acc, x):
    return x + acc


def _epi_swiglu(g, u):
    return jax.nn.silu(g) * u


def _epi_glu(acc, hs, b):
    return hs.astype(F32) * jax.nn.sigmoid(acc + b)


def _epi_mix(ssm_b, attn_b, ga, gs, ba, bs):
    return jax.nn.sigmoid(ga + ba) * attn_b + jax.nn.sigmoid(gs + bs) * ssm_b


def _qkv_kernel(q_ref, k_ref, v_ref, gq_ref, gk_ref, qn_ref, kf_ref, kb_ref, vf_ref, vb_ref, *, hd):
    gq = gq_ref[...] * (hd ** -0.5 * LOG2E)
    gk = gk_ref[...]
    for j in range(q_ref.shape[1] // hd):
        sl = slice(j * hd, (j + 1) * hd)
        q = q_ref[:, sl]
        qn = q * lax.rsqrt(jnp.mean(q * q, axis=-1, keepdims=True) + EPS) * gq
        qn_ref[:, sl] = qn.astype(BF16)
        k = k_ref[:, sl]
        kn = k * lax.rsqrt(jnp.mean(k * k, axis=-1, keepdims=True) + EPS) * gk
        kf_ref[:, sl] = kn
        kb_ref[:, sl] = kn.astype(BF16)
    v = v_ref[...]
    vf_ref[...] = v
    vb_ref[...] = v.astype(BF16)


def _qkv(proj, gq, gk, qk_width, v_width, hd):
    m = proj.shape[0]
    assert qk_width == v_width
    w = qk_width
    tm = _pick(m, (256, 128, 64, 32, 16, 8))
    spec = lambda c: pl.BlockSpec((tm, w), lambda i, c=c: (i, c))
    ospec = pl.BlockSpec((tm, w), lambda i: (i, 0))
    gspec = pl.BlockSpec((1, hd), lambda i: (0, 0))
    return pl.pallas_call(
        functools.partial(_qkv_kernel, hd=hd),
        grid=(m // tm,),
        in_specs=[spec(0), spec(1), spec(2), gspec, gspec],
        out_specs=[ospec] * 5,
        out_shape=[jax.ShapeDtypeStruct((m, w), dt) for dt in (BF16, F32, BF16, F32, BF16)],
        compiler_params=_cparams(("parallel",)),
    )(proj, proj, proj, gq.reshape(1, hd).astype(F32), gk.reshape(1, hd).astype(F32))


def _rel_bucket(rel, n_buckets):
    half = n_buckets // 2
    max_exact = half // 2
    n = jnp.abs(rel)
    nf = jnp.maximum(n, 1).astype(F32)
    large = max_exact + (jnp.log(nf / max_exact) / math.log(REL_MAX_DIST / max_exact)
                         * (half - max_exact)).astype(jnp.int32)
    large = jnp.minimum(large, half - 1)
    return jnp.where(rel > 0, half, 0) + jnp.where(n < max_exact, n, large)


def _bias_table(rel_bias, q_pos, k_pos):
    rel = k_pos[None, :] - q_pos[:, None]
    bucket = _rel_bucket(rel, rel_bias.shape[0])[None]
    table = rel_bias.astype(F32)
    bias = jnp.zeros((table.shape[1],) + rel.shape, F32)
    for b in range(table.shape[0]):
        bias = jnp.where(bucket == b, table[b][:, None, None], bias)
    visible = (k_pos[None, :] // CHUNK) <= (q_pos[:, None] // CHUNK)
    return jnp.where(visible[None], bias, NEG)


def _lambda(lam_ref, lam_init):
    lam = lam_ref[...]
    s1 = jnp.sum(lam[0:1] * lam[1:2], axis=-1, keepdims=True)
    s2 = jnp.sum(lam[2:3] * lam[3:4], axis=-1, keepdims=True)
    return jnp.exp(s1) - jnp.exp(s2) + lam_init


def _subln(o, g, lam_init):
    y = o * lax.rsqrt(jnp.mean(o * o, axis=-1, keepdims=True) + EPS)
    return (y * g) * (1.0 - lam_init)


_NT = (((1,), (1,)), ((), ()))


def _attn_prompt_kernel(qi_ref, kj_ref, ts_ref, q_ref, k_ref, v_ref, bias_ref, lam_ref, g_ref, o_ref,
                        m_sc, acc_sc, *, hd, lam_init):
    s = pl.program_id(2)
    qi = qi_ref[s]
    kj = kj_ref[s]

    @pl.when(kj == 0)
    def _():
        m_sc[...] = jnp.full_like(m_sc, NEG)
        acc_sc[...] = jnp.zeros_like(acc_sc)

    def step(with_bias):
        q = q_ref[...]
        k = k_ref[...]
        v = jnp.concatenate([v_ref[...], jnp.ones((k.shape[0], LANES), BF16)], axis=1)
        for c in range(2):
            sl = slice(c * hd, (c + 1) * hd)
            sc = lax.dot_general(q[:, sl], k[:, sl], _NT, preferred_element_type=F32)
            if with_bias:
                sc = sc + bias_ref[0, 0]
            m_prev = m_sc[c]
            m_new = jnp.maximum(m_prev, jnp.max(sc, axis=-1, keepdims=True))
            alpha = jnp.exp2(m_prev - m_new)
            p = jnp.exp2(sc - jnp.concatenate([m_new] * (sc.shape[1] // LANES), axis=1))
            acc_sc[c] = (jnp.concatenate([alpha] * (acc_sc.shape[2] // LANES), axis=1) * acc_sc[c]
                         + jnp.dot(p.astype(BF16), v, preferred_element_type=F32))
            m_sc[c] = m_new

    pl.when(ts_ref[s] < 2)(functools.partial(step, True))
    pl.when(ts_ref[s] == 2)(functools.partial(step, False))

    @pl.when(kj == qi)
    def _():
        lam = _lambda(lam_ref, lam_init)
        hw = 2 * hd
        o = []
        for c in range(2):
            acc = acc_sc[c]
            o.append(acc[:, :hw] / jnp.concatenate([acc[:, hw:]] * (hw // LANES), axis=1))
        o_ref[...] = _subln(o[0] - lam * o[1], g_ref[...], lam_init).astype(o_ref.dtype)


def _prompt_block(seq):
    blk = _pick(seq, (512, 256, 128))
    assert blk >= REL_MAX_DIST and blk % CHUNK == 0
    return blk


def _prompt_tables(rel_bias, seq):
    blk = _prompt_block(seq)
    pos = jnp.arange(blk, dtype=jnp.int32)
    far = _bias_table(rel_bias, pos[:1] + 2 * blk, pos[:1])
    near = jnp.stack([_bias_table(rel_bias, pos, pos), _bias_table(rel_bias, pos + blk, pos)])
    return jnp.where(near > 0.5 * NEG, (near - far[None]) * LOG2E, NEG)


def _attn_prompt(qn, kb, vb, bsz, seq, tables, lam4, g, n_heads, hd, lam_init):
    m, width = qn.shape
    hw = 2 * hd
    blk = _prompt_block(seq)
    nq = seq // blk
    pairs = [(i, j) for i in range(nq) for j in range(i + 1)]
    qi = jnp.array([p[0] for p in pairs], jnp.int32)
    kj = jnp.array([p[1] for p in pairs], jnp.int32)
    ts = jnp.array([min(p[0] - p[1], 2) for p in pairs], jnp.int32)
    grid_spec = pltpu.PrefetchScalarGridSpec(
        num_scalar_prefetch=3,
        grid=(bsz, n_heads, len(pairs)),
        in_specs=[
            pl.BlockSpec((blk, hw), lambda b, h, s, qi, kj, ts: (b * nq + qi[s], h)),
            pl.BlockSpec((blk, hw), lambda b, h, s, qi, kj, ts: (b * nq + kj[s], h)),
            pl.BlockSpec((blk, hw), lambda b, h, s, qi, kj, ts: (b * nq + kj[s], h)),
            pl.BlockSpec((1, 1, blk, blk), lambda b, h, s, qi, kj, ts: (jnp.minimum(ts[s], 1), h, 0, 0)),
            pl.BlockSpec((4, hd), lambda b, h, s, qi, kj, ts: (0, 0)),
            pl.BlockSpec((1, hw), lambda b, h, s, qi, kj, ts: (0, 0)),
        ],
        out_specs=pl.BlockSpec((blk, hw), lambda b, h, s, qi, kj, ts: (b * nq + qi[s], h)),
        scratch_shapes=[pltpu.VMEM((2, blk, LANES), F32),
                        pltpu.VMEM((2, blk, hw + LANES), F32)],
    )
    return pl.pallas_call(
        functools.partial(_attn_prompt_kernel, hd=hd, lam_init=lam_init),
        grid_spec=grid_spec,
        out_shape=jax.ShapeDtypeStruct((m, width), BF16),
        compiler_params=_cparams(("parallel", "parallel", "arbitrary")),
    )(qi, kj, ts, qn, kb, vb, tables, lam4, g)


def _attn_sample_kernel(q_ref, ck_ref, cv_ref, nk_ref, nv_ref, bp_ref, bn_ref, lam_ref, g_ref, o_in, o_ref,
                        *, hd, n_heads, past, lam_init):
    del o_in
    lam = _lambda(lam_ref, lam_init)
    hw = 2 * hd
    for h in range(n_heads):
        cols = slice(h * hw, (h + 1) * hw)
        q = q_ref[:, cols]
        nk = nk_ref[:, cols]
        nv = nv_ref[:, cols]
        cv = cv_ref[0, 0, :, cols].astype(BF16)
        bp = bp_ref[h]
        bn = bn_ref[h]
        probs = []
        for c in range(2):
            sl = slice(c * hd, (c + 1) * hd)
            ck = ck_ref[0, 0, pl.ds(2 * h + c, past, stride=2 * n_heads), :].astype(BF16)
            sp = lax.dot_general(q[:, sl], ck, _NT, preferred_element_type=F32) + bp
            sn = lax.dot_general(q[:, sl], nk[:, sl], _NT, preferred_element_type=F32) + bn
            m = jnp.maximum(jnp.max(sp, axis=-1, keepdims=True), jnp.max(sn, axis=-1, keepdims=True))
            pp = jnp.exp2(sp - m)
            pn = jnp.exp2(sn - m)
            inv = 1.0 / (jnp.sum(pp, axis=-1, keepdims=True) + jnp.sum(pn, axis=-1, keepdims=True))
            probs.append((pp * inv, pn * inv))
        wp = probs[0][0] - lam * probs[1][0]
        wn = probs[0][1] - lam * probs[1][1]
        o = (jnp.dot(wp.astype(BF16), cv, preferred_element_type=F32)
             + jnp.dot(wn.astype(BF16), nv, preferred_element_type=F32))
        o_ref[:, cols] = _subln(o, g_ref[...], lam_init).astype(o_ref.dtype)


def _sample_table(rel_bias, past, s_len):
    table = _bias_table(rel_bias, past + jnp.arange(s_len, dtype=jnp.int32),
                        jnp.arange(past + s_len, dtype=jnp.int32))
    return jnp.where(table > 0.5 * NEG, table * LOG2E, NEG)


def _attn_sample(qn, kb, vb, o, row0, bsz, s_len, cache_k, cache_v, layer, table, lam4, g, n_heads, hd, lam_init):
    width = qn.shape[1]
    hw = 2 * hd
    past = cache_v.shape[2]
    assert row0 % s_len == 0
    new_spec = pl.BlockSpec((s_len, width), lambda b: (row0 // s_len + b, 0))
    return pl.pallas_call(
        functools.partial(_attn_sample_kernel, hd=hd, n_heads=n_heads, past=past, lam_init=lam_init),
        grid=(bsz,),
        in_specs=[new_spec,
                  pl.BlockSpec((1, 1, past * n_heads * 2, hd), lambda b: (layer, b, 0, 0)),
                  pl.BlockSpec((1, 1, past, width), lambda b: (layer, b, 0, 0)),
                  new_spec, new_spec,
                  pl.BlockSpec((n_heads, s_len, past), lambda b: (0, 0, 0)),
                  pl.BlockSpec((n_heads, s_len, s_len), lambda b: (0, 0, 0)),
                  pl.BlockSpec((4, hd), lambda b: (0, 0)),
                  pl.BlockSpec((1, hw), lambda b: (0, 0)),
                  pl.BlockSpec(memory_space=pl.ANY)],
        out_specs=new_spec,
        out_shape=jax.ShapeDtypeStruct(o.shape, o.dtype),
        input_output_aliases={9: 0},
        compiler_params=_cparams(("parallel",)),
    )(qn, cache_k, cache_v, kb, vb, table[:, :, :past], table[:, :, past:], lam4, g, o)


def _split(x):
    hi = x.astype(BF16)
    lo = (x - hi.astype(F32)).astype(BF16)
    return hi, lo


def _dot3(a, b):
    ah, al = _split(a)
    bh, bl = _split(b)
    d = functools.partial(jnp.dot, preferred_element_type=F32)
    return d(ah, bh) + (d(ah, bl) + d(al, bh))


def _s5_operators(a_re, a_im, log_dt, b_re, b_im, c_re, c_im, d):
    hp = lax.Precision.HIGHEST
    g, n = a_re.shape
    c = b_re.shape[-1]
    t = S5_T
    assert 2 * n == LANES and LANES % c == 0
    dt = jnp.exp(log_dt.astype(F32))[:, None]
    lr = a_re.astype(F32)
    li = a_im.astype(F32)
    mag = jnp.exp(lr * dt)
    ab_re = mag * jnp.cos(li * dt)
    ab_im = mag * jnp.sin(li * dt)
    den = lr * lr + li * li
    num_re = ab_re - 1.0
    coef_re = (num_re * lr + ab_im * li) / den
    coef_im = (ab_im * lr - num_re * li) / den
    br = b_re.astype(F32)
    bi = b_im.astype(F32)
    bb_re = coef_re[..., None] * br - coef_im[..., None] * bi
    bb_im = coef_re[..., None] * bi + coef_im[..., None] * br
    pr = [jnp.ones_like(ab_re)]
    pi = [jnp.zeros_like(ab_re)]
    for _ in range(t):
        pr.append(pr[-1] * ab_re - pi[-1] * ab_im)
        pi.append(pr[-2] * ab_im + pi[-1] * ab_re)
    p_re = jnp.stack(pr)
    p_im = jnp.stack(pi)
    cr = c_re.astype(F32)
    ci = c_im.astype(F32)
    cp_re = cr[None] * p_re[:, :, None, :] - ci[None] * p_im[:, :, None, :]
    cp_im = cr[None] * p_im[:, :, None, :] + ci[None] * p_re[:, :, None, :]
    kk = (jnp.einsum("jgon,gni->gjoi", cp_re[:t], bb_re, precision=hp)
          - jnp.einsum("jgon,gni->gjoi", cp_im[:t], bb_im, precision=hp))
    oc = LANES // c
    go = g // oc
    w = jnp.einsum("ogjci,gh->ojgihc", kk.reshape(go, oc, t, c, c), jnp.eye(oc, dtype=F32))
    w = w.reshape(go, t, LANES, LANES).astype(BF16)
    pw_re = p_re[:t][::-1]
    pw_im = p_im[:t][::-1]
    min_re = pw_re[:, :, :, None] * bb_re[None] - pw_im[:, :, :, None] * bb_im[None]
    min_im = pw_re[:, :, :, None] * bb_im[None] + pw_im[:, :, :, None] * bb_re[None]
    m_in = jnp.concatenate([jnp.transpose(min_re, (1, 0, 3, 2)), jnp.transpose(min_im, (1, 0, 3, 2))], axis=-1)
    m_in = jnp.transpose(m_in.reshape(go, oc, t, c, 2 * n), (0, 2, 1, 3, 4)).reshape(go, t * LANES, 2 * n)
    m_out = jnp.concatenate([cp_re[1:], -cp_im[1:]], axis=-1)
    m_out = jnp.transpose(m_out.reshape(t, go, oc, c, 2 * n), (1, 0, 2, 3, 4)).reshape(go, t * LANES, 2 * n)
    a_t_re, a_t_im = p_re[t], p_im[t]
    a1 = jnp.concatenate([a_t_re, a_t_re], axis=-1).reshape(1, g * 2 * n)
    a2 = jnp.concatenate([-a_t_im, a_t_im], axis=-1).reshape(1, g * 2 * n)
    return dict(w=w, m_in=m_in, m_out=m_out, a1=a1, a2=a2, d_row=d.astype(F32).reshape(1, g * c))


def _toeplitz_kernel(w_ref, o_ref, *, t):
    zero = jnp.zeros((LANES, LANES), o_ref.dtype)
    for s in range(t):
        for u in range(t):
            o_ref[0, s * LANES:(s + 1) * LANES, u * LANES:(u + 1) * LANES] = w_ref[0, u - s] if u >= s else zero


def _toeplitz(w):
    go, t = w.shape[:2]
    return pl.pallas_call(
        functools.partial(_toeplitz_kernel, t=t),
        grid=(go,),
        in_specs=[pl.BlockSpec((1, t, LANES, LANES), lambda o: (o, 0, 0, 0))],
        out_specs=pl.BlockSpec((1, t * LANES, t * LANES), lambda o: (o, 0, 0)),
        out_shape=jax.ShapeDtypeStruct((go, t * LANES, t * LANES), w.dtype),
        compiler_params=_cparams(("parallel",)),
    )(w)


def _expand(compact, c):
    oc = LANES // c
    tiled = jnp.concatenate([compact] * oc, axis=1)
    r = lax.broadcasted_iota(jnp.int32, tiled.shape, 0)
    q = lax.broadcasted_iota(jnp.int32, tiled.shape, 1)
    own = ((r & (LANES - 1)) >> int(math.log2(c))) == (q >> int(math.log2(LANES)))
    return jnp.where(own, tiled, 0.0)


def _gather_chunks(u_ref, xcat, t, rows):
    for s in range(t):
        xcat[:, s * LANES:(s + 1) * LANES] = u_ref[pl.ds(s, rows, stride=t), :]


def _s5_state_kernel(u_ref, min_ref, v_ref, xcat, *, t, rows, c):
    _gather_chunks(u_ref, xcat, t, rows)
    v_ref[...] = _dot3(xcat[...], _expand(min_ref[0], c))


def _s5_scan_kernel(v_ref, a1_ref, a2_ref, h0_ref, h_ref, f_ref, vs_sc, *, nchunk, half):
    a1 = a1_ref[...]
    a2 = a2_ref[...]
    lt = a1.shape[1]

    def swap(x):
        lane = lax.broadcasted_iota(jnp.int32, x.shape, 1)
        is_re = (lane & (2 * half - 1)) < half
        return jnp.where(is_re, pltpu.roll(x, lt - half, axis=1), pltpu.roll(x, half, axis=1))

    vs_sc[...] = swap(v_ref[...])
    a2s = swap(a2)
    h0 = h0_ref[...]

    def body(k, carry):
        h, g = carry
        h_ref[pl.ds(k, 1), :] = h
        return (a1 * h + a2 * g + v_ref[pl.ds(k, 1), :], a1 * g + a2s * h + vs_sc[pl.ds(k, 1), :])

    h, _ = lax.fori_loop(0, nchunk, body, (h0, swap(h0)))
    f_ref[...] = h


def _s5_out_kernel(u_ref, mi_ref, mo_ref, h_ref, d_ref, o_in, o_ref, xcat, *, t, rows, c):
    del o_in
    _gather_chunks(u_ref, xcat, t, rows)
    mo = _expand(mo_ref[0], c).astype(BF16)
    y = (jnp.dot(xcat[...].astype(BF16), mi_ref[0], preferred_element_type=F32)
         + lax.dot_general(h_ref[...].astype(BF16), mo, _NT, preferred_element_type=F32))
    d = d_ref[...]
    for s in range(t):
        sl = slice(s * LANES, (s + 1) * LANES)
        o_ref[pl.ds(s, rows, stride=t), :] = jax.nn.gelu(y[:, sl] + xcat[:, sl] * d)


def _s5(proj, hs, row0, bsz, seq, ucol, h0, ops, mi, groups):
    t = S5_T
    width = ops["d_row"].shape[1]
    c = width // groups
    gl = ops["a1"].shape[1]
    noct = width // LANES
    ocl = gl // noct
    nchunk = seq // t
    tokens = bsz * seq
    tb = _pick(tokens, (8192, 4096, 2048, 1024, 512, 256, 128))
    rb = tb // t
    assert row0 % tb == 0 and ucol % LANES == 0 and seq % t == 0
    u_spec = pl.BlockSpec((tb, LANES), lambda o, r: (row0 // tb + r, ucol // LANES + o))
    st_spec = pl.BlockSpec((rb, ocl), lambda o, r: (r, o))
    cmp_spec = pl.BlockSpec((1, t * LANES, LANES), lambda o, r: (o, 0, 0))
    v = pl.pallas_call(
        functools.partial(_s5_state_kernel, t=t, rows=rb, c=c),
        grid=(noct, tokens // tb),
        in_specs=[u_spec, cmp_spec],
        out_specs=st_spec,
        out_shape=jax.ShapeDtypeStruct((tokens // t, gl), F32),
        scratch_shapes=[pltpu.VMEM((rb, t * LANES), F32)],
        compiler_params=_cparams(("parallel", "parallel")),
    )(proj, ops["m_in"])

    lt = _pick(gl, (2048, 1024, 512, 256, 128))
    seq_blk = pl.BlockSpec((None, nchunk, lt), lambda b, j: (b, 0, j))
    vec_blk = pl.BlockSpec((1, lt), lambda b, j: (0, j))
    st_blk = pl.BlockSpec((None, 1, lt), lambda b, j: (b, 0, j))
    h, f = pl.pallas_call(
        functools.partial(_s5_scan_kernel, nchunk=nchunk, half=LANES // 2),
        grid=(bsz, gl // lt),
        in_specs=[seq_blk, vec_blk, vec_blk, st_blk],
        out_specs=[seq_blk, st_blk],
        out_shape=[jax.ShapeDtypeStruct((bsz, nchunk, gl), F32), jax.ShapeDtypeStruct((bsz, 1, gl), F32)],
        scratch_shapes=[pltpu.VMEM((nchunk, lt), F32)],
        compiler_params=_cparams(("parallel", "parallel")),
    )(v.reshape(bsz, nchunk, gl), ops["a1"], ops["a2"], h0.reshape(bsz, 1, gl))

    hs = pl.pallas_call(
        functools.partial(_s5_out_kernel, t=t, rows=rb, c=c),
        grid=(noct, tokens // tb),
        in_specs=[u_spec,
                  pl.BlockSpec((1, t * LANES, t * LANES), lambda o, r: (o, 0, 0)),
                  cmp_spec, st_spec,
                  pl.BlockSpec((1, LANES), lambda o, r: (0, o)),
                  pl.BlockSpec(memory_space=pl.ANY)],
        out_specs=pl.BlockSpec((tb, LANES), lambda o, r: (row0 // tb + r, o)),
        out_shape=jax.ShapeDtypeStruct(hs.shape, hs.dtype),
        input_output_aliases={5: 0},
        scratch_shapes=[pltpu.VMEM((rb, t * LANES), F32)],
        compiler_params=_cparams(("parallel", "parallel")),
    )(proj, mi, ops["m_out"], h.reshape(tokens // t, gl), ops["d_row"], hs)
    return hs, f.reshape(bsz, gl)


def _interleave(re, im):
    return jnp.concatenate([re.astype(F32), im.astype(F32)], axis=-1).reshape(re.shape[0], -1)


def _deinterleave(h, groups):
    h = h.reshape(h.shape[0], groups, 2, -1)
    return h[:, :, 0], h[:, :, 1]


def _router_kernel(x_ref, w_ref, b_ref, o_ref, cnt_ref, carry):
    i = pl.program_id(0)

    @pl.when(i == 0)
    def _():
        carry[...] = jnp.zeros_like(carry)

    tm = x_ref.shape[0]
    logits = jnp.dot(x_ref[...].astype(BF16), w_ref[...], preferred_element_type=F32) + b_ref[...]
    lane = lax.broadcasted_iota(jnp.int32, logits.shape, 1)
    m1 = jnp.max(logits, axis=-1, keepdims=True)
    i1 = jnp.min(jnp.where(logits == m1, lane, LANES), axis=-1, keepdims=True)
    rest = jnp.where(lane == i1, -3e38, logits)
    m2 = jnp.max(rest, axis=-1, keepdims=True)
    i2 = jnp.min(jnp.where(rest == m2, lane, LANES), axis=-1, keepdims=True)
    e = jnp.exp(m2 - m1)
    w1 = 1.0 / (1.0 + e)
    w2 = e / (1.0 + e)
    onehot = jnp.where((lane == i1) | (lane == i2), 1.0, 0.0)
    r = lax.broadcasted_iota(jnp.int32, (tm, tm), 0)
    c = lax.broadcasted_iota(jnp.int32, (tm, tm), 1)
    tri = jnp.where(c < r, 1.0, 0.0).astype(BF16)
    rank = jnp.dot(tri, onehot.astype(BF16), preferred_element_type=F32) + carry[...]
    r1 = jnp.sum(jnp.where(lane == i1, rank, 0.0), axis=-1, keepdims=True)
    r2 = jnp.sum(jnp.where(lane == i2, rank, 0.0), axis=-1, keepdims=True)
    carry[...] += jnp.sum(onehot, axis=0, keepdims=True)
    cols = (i1.astype(F32), i2.astype(F32), w1, w2, r1, r2)
    out = jnp.zeros(logits.shape, F32)
    for idx, col in enumerate(cols):
        out = jnp.where(lane == idx, col, out)
    o_ref[...] = out
    cnt_ref[...] = carry[...]


def _router(xn, w_router, b_router):
    m, d = xn.shape
    e = w_router.shape[1]
    assert e <= LANES
    tm = _pick(m, (512, 256, 128, 64, 32, 16, 8))
    w = jnp.zeros((d, LANES), BF16).at[:, :e].set(w_router.astype(BF16))
    b = jnp.full((1, LANES), NEG, F32).at[0, :e].set(b_router.astype(F32))
    return pl.pallas_call(
        _router_kernel,
        grid=(m // tm,),
        in_specs=[pl.BlockSpec((tm, d), lambda i: (i, 0)), pl.BlockSpec((d, LANES), lambda i: (0, 0)),
                  pl.BlockSpec((1, LANES), lambda i: (0, 0))],
        out_specs=[pl.BlockSpec((tm, LANES), lambda i: (i, 0)), pl.BlockSpec((1, LANES), lambda i: (0, 0))],
        out_shape=[jax.ShapeDtypeStruct((m, LANES), F32), jax.ShapeDtypeStruct((1, LANES), F32)],
        scratch_shapes=[pltpu.VMEM((1, LANES), F32)],
        compiler_params=_cparams(("arbitrary",)),
    )(xn, w, b)


def _row_copy(src, dst, src_row, dst_row, sem):
    return pltpu.make_async_copy(src.at[pl.ds(src_row, 1)], dst.at[pl.ds(dst_row, 1)], sem)


def _dispatch_kernel(pos_ref, x_ref, zeros_hbm, xs_hbm, sem, *, tb):
    del zeros_hbm
    base = pl.program_id(0) * tb

    def issue(t, carry):
        for s in range(2):
            _row_copy(x_ref, xs_hbm, t, pos_ref[2 * (base + t) + s], sem).start()
        return carry

    lax.fori_loop(0, tb, issue, 0)

    def drain(t, carry):
        for s in range(2):
            _row_copy(x_ref, xs_hbm, 0, 0, sem).wait()
        return carry

    lax.fori_loop(0, tb, drain, 0)


def _dispatch(xn, pos, rows_padded):
    m, d = xn.shape
    tb = _pick(m, (512, 256, 128, 64, 32, 16, 8))
    grid_spec = pltpu.PrefetchScalarGridSpec(
        num_scalar_prefetch=1,
        grid=(m // tb,),
        in_specs=[pl.BlockSpec((tb, d), lambda i, pos: (i, 0)), pl.BlockSpec(memory_space=pl.ANY)],
        out_specs=pl.BlockSpec(memory_space=pl.ANY),
        scratch_shapes=[pltpu.SemaphoreType.DMA(())],
    )
    return pl.pallas_call(
        functools.partial(_dispatch_kernel, tb=tb),
        grid_spec=grid_spec,
        out_shape=jax.ShapeDtypeStruct((rows_padded, d), xn.dtype),
        input_output_aliases={2: 0},
        compiler_params=pltpu.CompilerParams(dimension_semantics=("arbitrary",), has_side_effects=True,
                                             vmem_limit_bytes=VMEM_LIMIT),
    )(pos, xn, jnp.zeros((rows_padded, d), xn.dtype))


def _combine_kernel(pos_ref, ys_hbm, x_ref, r_ref, o_ref, buf, sem, *, tb):
    base = pl.program_id(0) * tb

    def issue(t, carry):
        for s in range(2):
            _row_copy(ys_hbm, buf.at[s], pos_ref[2 * (base + t) + s], t, sem).start()
        return carry

    lax.fori_loop(0, tb, issue, 0)

    def drain(t, carry):
        for s in range(2):
            _row_copy(ys_hbm, buf.at[s], 0, 0, sem).wait()
        return carry

    lax.fori_loop(0, tb, drain, 0)
    r = r_ref[...]
    o_ref[...] = x_ref[...] + (r[:, 2:3] * buf[0] + r[:, 3:4] * buf[1])


def _combine(ys, pos, x, routing):
    m, d = x.shape
    tb = _pick(m, (256, 128, 64, 32, 16, 8))
    grid_spec = pltpu.PrefetchScalarGridSpec(
        num_scalar_prefetch=1,
        grid=(m // tb,),
        in_specs=[pl.BlockSpec(memory_space=pl.ANY),
                  pl.BlockSpec((tb, d), lambda i, pos: (i, 0)),
                  pl.BlockSpec((tb, LANES), lambda i, pos: (i, 0))],
        out_specs=pl.BlockSpec((tb, d), lambda i, pos: (i, 0)),
        scratch_shapes=[pltpu.VMEM((2, tb, d), F32), pltpu.SemaphoreType.DMA(())],
    )
    return pl.pallas_call(
        functools.partial(_combine_kernel, tb=tb),
        grid_spec=grid_spec,
        out_shape=jax.ShapeDtypeStruct((m, d), F32),
        compiler_params=_cparams(("arbitrary",)),
    )(pos, ys, x, routing)


def _moe(x, xn, w_router, b_router, wg, wu, wd):
    m, d = x.shape
    n_exp = wg.shape[0]
    tg = 512 if m >= 4096 else 32
    n_tiles = -(-(2 * m + n_exp * (tg - 1)) // tg)
    rows_padded = n_tiles * tg
    routing, counts = _router(xn, w_router, b_router)
    cnt = counts[0, :n_exp].astype(jnp.int32)
    padded = ((cnt + tg - 1) // tg) * tg
    ends = jnp.cumsum(padded)
    offs = ends - padded
    lanes = jnp.arange(LANES, dtype=jnp.int32)
    offs_row = jnp.zeros((LANES,), jnp.int32).at[:n_exp].set(offs)

    def row_of(col_expert, col_rank):
        e = routing[:, col_expert].astype(jnp.int32)
        off = jnp.sum(jnp.where(e[:, None] == lanes[None, :], offs_row[None, :], 0), axis=1)
        return off + routing[:, col_rank].astype(jnp.int32)

    pos = jnp.stack([row_of(0, 4), row_of(1, 5)], axis=1).reshape(-1)
    n_active = (ends[-1] // tg).reshape(1).astype(jnp.int32)
    tile_start = jnp.arange(n_tiles, dtype=jnp.int32) * tg
    tile_expert = jnp.minimum(jnp.sum(tile_start[:, None] >= ends[None, :], axis=1), n_exp - 1).astype(jnp.int32)
    xs = _dispatch(xn, pos, rows_padded)
    group = (tile_expert, n_active)
    f = wg.shape[-1]
    h = _matmul(xs, [wg, wu], _epi_swiglu, BF16, group=group, tm=tg,
                tn=_pick(f, (1408, 1024, 512, 256, 128)), tk=_pick(d, (2048, 1024, 512, 256, 128)))
    ys = _matmul(h, [wd], _epi_id, F32, group=group, tm=tg, tk=f)
    return _combine(ys, pos, x, routing)


def kernel(x_prompt, x_sample, cache_k, cache_v, state_ssm_re, state_ssm_im, rel_bias, norm_mix_g, w_in, b_gate, q_norm_g, k_norm_g, lambda_q1, lambda_k1, lambda_q2, lambda_k2, subln_g, w_attn_proj, ssm_a_re, ssm_a_im, ssm_log_dt, ssm_b_re, ssm_b_im, ssm_c_re, ssm_c_im, ssm_d, w_glu, b_glu, w_ssm_proj, w_out, norm_ffn_g, w_ff_gate, w_ff_up, w_ff_down, w_router, b_router, w_e_gate, w_e_up, w_e_down):
    bp, lp, d = x_prompt.shape
    bs, ls, _ = x_sample.shape
    depth = w_in.shape[0]
    past = cache_k.shape[2]
    n_heads, hd = cache_k.shape[3], cache_k.shape[5]
    qkw = n_heads * 2 * hd
    groups, n_state = state_ssm_re.shape[2], state_ssm_re.shape[3]
    ssm_w = ssm_d.shape[1]
    mp, ms = bp * lp, bs * ls
    assert hd % LANES == 0 and lp % S5_T == 0 and ls % S5_T == 0 and ssm_w % LANES == 0

    ck = cache_k.reshape(depth, bs, past * n_heads * 2, hd)
    cv = cache_v.reshape(depth, bs, past, qkw)
    x = jnp.concatenate([x_prompt.reshape(mp, d), x_sample.reshape(ms, d)], axis=0)
    m = mp + ms

    d_ff = w_ff_gate.shape[-1]
    ff_pad = (-d_ff) % 1024 if d_ff > 1024 else 0
    outs = {k: [] for k in ("kp", "vp", "hrp", "hip", "ks", "vs", "hrs", "his")}
    zeros_state = jnp.zeros((bp, groups * 2 * n_state), F32)
    tables_p = _prompt_tables(rel_bias, lp)
    table_s = _sample_table(rel_bias, past, ls)

    for l in range(depth):
        lam_init = 0.8 - 0.6 * math.exp(-0.3 * l)
        xn = _rmsnorm(x, norm_mix_g[l], BF16)
        proj = _matmul(xn, [w_in[l].astype(BF16)], _epi_id, F32)
        qn, kf, kb, vf, vb = _qkv(proj, q_norm_g[l], k_norm_g[l], qkw, qkw, hd)
        lam4 = jnp.stack([lambda_q1[l], lambda_k1[l], lambda_q2[l], lambda_k2[l]]).astype(F32)
        g_sub = subln_g[l].reshape(1, 2 * hd).astype(F32)
        o = _attn_prompt(qn, kb, vb, bp, lp, tables_p, lam4, g_sub, n_heads, hd, lam_init)
        o = _attn_sample(qn, kb, vb, o, mp, bs, ls, ck, cv, l, table_s, lam4, g_sub, n_heads, hd, lam_init)
        attn_b = _matmul(o, [w_attn_proj[l].astype(BF16)], _epi_id, F32)

        ops = _s5_operators(ssm_a_re[l], ssm_a_im[l], ssm_log_dt[l], ssm_b_re[l], ssm_b_im[l],
                            ssm_c_re[l], ssm_c_im[l], ssm_d[l])
        mi = _toeplitz(ops["w"])
        hs = jnp.zeros((m, ssm_w), F32)
        hs, h_p = _s5(proj, hs, 0, bp, lp, 3 * qkw, zeros_state, ops, mi, groups)
        hs, h_s = _s5(proj, hs, mp, bs, ls, 3 * qkw, _interleave(state_ssm_re[l], state_ssm_im[l]), ops, mi, groups)
        hr_p, hi_p = _deinterleave(h_p, groups)
        hr_s, hi_s = _deinterleave(h_s, groups)
        hs = _matmul(hs, [w_glu[l].astype(BF16)], _epi_glu, BF16,
                     extras=[(hs, "tile", 0), (b_glu[l].reshape(1, -1).astype(F32), "row", 0)])
        tn = _pick(math.gcd(d, 3 * qkw + ssm_w), (1024, 512, 256, 128))
        goff = (3 * qkw + ssm_w) // tn
        bg = b_gate[l].reshape(1, -1).astype(F32)
        mix = _matmul(hs, [w_ssm_proj[l].astype(BF16)], _epi_mix, BF16, tn=tn,
                      extras=[(attn_b, "tile", 0), (proj, "tile", goff), (proj, "tile", goff + d // tn),
                              (bg, "row", 0), (bg, "row", d // tn)])
        x = _matmul(mix, [w_out[l].astype(BF16)], _epi_residual, F32, extras=[(x, "tile", 0)])

        i = l // 2
        if l % 2 == 0:
            xn = _rmsnorm(x, norm_ffn_g[l], BF16)
            wg = jnp.pad(w_ff_gate[i].astype(BF16), ((0, 0), (0, ff_pad)))
            wu = jnp.pad(w_ff_up[i].astype(BF16), ((0, 0), (0, ff_pad)))
            wd = jnp.pad(w_ff_down[i].astype(BF16), ((0, ff_pad), (0, 0)))
            h = _matmul(xn, [wg, wu], _epi_swiglu, BF16)
            x = _matmul(h, [wd], _epi_residual, F32, extras=[(x, "tile", 0)])
        else:
            xn = _rmsnorm(x, norm_ffn_g[l], F32)
            x = _moe(x, xn, w_router[i], b_router[i], w_e_gate[i].astype(BF16), w_e_up[i].astype(BF16),
                     w_e_down[i].astype(BF16))

        outs["kp"].append(kf[:mp].reshape(bp, lp, n_heads, 2, hd))
        outs["vp"].append(vf[:mp].reshape(bp, lp, n_heads, 2 * hd))
        outs["hrp"].append(hr_p)
        outs["hip"].append(hi_p)
        outs["ks"].append(kf[mp:].reshape(bs, ls, n_heads, 2, hd))
        outs["vs"].append(vf[mp:].reshape(bs, ls, n_heads, 2 * hd))
        outs["hrs"].append(hr_s)
        outs["his"].append(hi_s)

    st = {k: jnp.stack(v) for k, v in outs.items()}
    return (x[:mp].reshape(bp, lp, d), x[mp:].reshape(bs, ls, d), st["kp"], st["vp"], st["hrp"], st["hip"],
            st["ks"], st["vs"], st["hrs"], st["his"])
```

```python
import functools
import math

import jax
import jax.numpy as jnp
from jax import lax
from jax.experimental import pallas as pl
from jax.experimental.pallas import tpu as pltpu

F32 = jnp.float32
BF16 = jnp.bfloat16

CHUNK = 64
REL_MAX_DIST = 128
EPS = 1e-6
NEG = -1e30
LANES = 128
VMEM_LIMIT = 56 * 1024 * 1024
MM_VMEM_BUDGET = 50 * 1024 * 1024
S5_T = 16
LOG2E = math.log2(math.e)


def _pick(n, cands):
    for c in cands:
        if n % c == 0:
            return c
    return n


def _cparams(sem):
    return pltpu.CompilerParams(dimension_semantics=sem, vmem_limit_bytes=VMEM_LIMIT)


def _rmsnorm_kernel(x_ref, g_ref, o_ref):
    x = x_ref[...]
    y = x * lax.rsqrt(jnp.mean(x * x, axis=-1, keepdims=True) + EPS)
    o_ref[...] = (y * g_ref[...]).astype(o_ref.dtype)


def _rmsnorm(x, g, out_dtype):
    m, d = x.shape
    tm = _pick(m, (512, 256, 128, 64, 32, 16, 8))
    return pl.pallas_call(
        _rmsnorm_kernel,
        grid=(m // tm,),
        in_specs=[pl.BlockSpec((tm, d), lambda i: (i, 0)), pl.BlockSpec((1, d), lambda i: (0, 0))],
        out_specs=pl.BlockSpec((tm, d), lambda i: (i, 0)),
        out_shape=jax.ShapeDtypeStruct((m, d), out_dtype),
        compiler_params=_cparams(("parallel",)),
    )(x, g.reshape(1, d).astype(F32))


def _mm_kernel(*refs, nb, nx, nk, epi, grouped):
    if grouped:
        nact_ref = refs[1]
        refs = refs[2:]
    a_ref = refs[0]
    b_refs = refs[1:1 + nb]
    x_refs = refs[1 + nb:1 + nb + nx]
    o_ref = refs[1 + nb + nx]
    acc_refs = refs[2 + nb + nx:]

    def body():
        a = a_ref[...].astype(BF16)
        if nk == 1:
            accs = [jnp.dot(a, b[...], preferred_element_type=F32) for b in b_refs]
            o_ref[...] = epi(*accs, *[x[...] for x in x_refs]).astype(o_ref.dtype)
            return
        k = pl.program_id(2)

        @pl.when(k == 0)
        def _():
            for acc in acc_refs:
                acc[...] = jnp.zeros_like(acc)

        for acc, b in zip(acc_refs, b_refs):
            acc[...] += jnp.dot(a, b[...], preferred_element_type=F32)

        @pl.when(k == nk - 1)
        def _():
            o_ref[...] = epi(*[acc[...] for acc in acc_refs], *[x[...] for x in x_refs]).astype(o_ref.dtype)

    if grouped:
        active = pl.program_id(0) < nact_ref[0]
        pl.when(active)(body)

        @pl.when(jnp.logical_not(active))
        def _():
            o_ref[...] = jnp.zeros_like(o_ref)
    else:
        body()


def _mm_tiles(n, kdim, tm, tn, tk, nb, a_bytes, out_bytes, extras):
    def vmem(tn_, tk_):
        fixed = (2 * tm * tn_ * out_bytes + nb * tm * tn_ * 4
                 + sum(2 * (tm if kind == "tile" else 1) * tn_ * arr.dtype.itemsize for arr, kind, _ in extras))
        return fixed + tk_ * (2 * (tm * a_bytes + nb * tn_ * 2) + (tm * 2 if a_bytes > 2 else 0))

    tns = [tn] if tn else [c for c in (1024, 512, 256, 128) if n % c == 0] or [n]
    tks = [tk] if tk else [c for c in (kdim, 2816, 2048, 1024, 512, 256, 128) if kdim % c == 0]
    for tn_ in tns[:2]:
        if vmem(tn_, tks[0]) <= MM_VMEM_BUDGET:
            return tn_, tks[0]
    tn_ = tns[0] if nb == 1 else tns[min(1, len(tns) - 1)]
    return tn_, next((c for c in tks if vmem(tn_, c) <= MM_VMEM_BUDGET), tks[-1])


def _matmul(a, bs, epi, out_dtype, extras=(), group=None, tm=None, tn=None, tk=None):
    m, kdim = a.shape
    n = bs[0].shape[-1]
    tm = tm or _pick(m, (1024, 512, 256, 128, 64, 32, 16, 8))
    if tn is None or tk is None:
        tn, tk = _mm_tiles(n, kdim, tm, tn, tk, len(bs), a.dtype.itemsize, jnp.dtype(out_dtype).itemsize, extras)
    nk = kdim // tk
    grouped = group is not None

    if grouped:
        def row(i, te, na):
            return jnp.minimum(i, na[0] - 1)
        a_spec = pl.BlockSpec((tm, tk), lambda i, j, k, te, na: (row(i, te, na), k))
        b_specs = [pl.BlockSpec((None, tk, tn), lambda i, j, k, te, na: (te[row(i, te, na)], k, j)) for _ in bs]
        o_spec = pl.BlockSpec((tm, tn), lambda i, j, k, te, na: (i, j))
        x_specs = []
        assert not extras
    else:
        a_spec = pl.BlockSpec((tm, tk), lambda i, j, k: (i, k))
        b_specs = [pl.BlockSpec((tk, tn), lambda i, j, k: (k, j)) for _ in bs]
        o_spec = pl.BlockSpec((tm, tn), lambda i, j, k: (i, j))
        x_specs = []
        for arr, kind, off in extras:
            if kind == "tile":
                x_specs.append(pl.BlockSpec((tm, tn), lambda i, j, k, off=off: (i, j + off)))
            else:
                x_specs.append(pl.BlockSpec((1, tn), lambda i, j, k, off=off: (0, j + off)))

    scratch = [pltpu.VMEM((tm, tn), F32) for _ in bs] if nk > 1 else []
    kern = functools.partial(_mm_kernel, nb=len(bs), nx=len(extras), nk=nk, epi=epi, grouped=grouped)
    grid_spec = pltpu.PrefetchScalarGridSpec(
        num_scalar_prefetch=2 if grouped else 0,
        grid=(m // tm, n // tn, nk),
        in_specs=[a_spec, *b_specs, *x_specs],
        out_specs=o_spec,
        scratch_shapes=scratch,
    )
    args = ([group[0], group[1]] if grouped else []) + [a, *bs, *[e[0] for e in extras]]
    return pl.pallas_call(
        kern,
        grid_spec=grid_spec,
        out_shape=jax.ShapeDtypeStruct((m, n), out_dtype),
        compiler_params=_cparams(("arbitrary" if grouped else "parallel", "arbitrary" if grouped else "parallel",
                                  "arbitrary")),
    )(*args)


def _epi_id(acc):
    return acc


def _epi_residual(acc, x):
    return x + acc


def _epi_swiglu(g, u):
    return jax.nn.silu(g) * u


def _epi_glu(acc, hs, b):
    return hs.astype(F32) * jax.nn.sigmoid(acc + b)


def _epi_mix(ssm_b, attn_b, ga, gs, ba, bs):
    return jax.nn.sigmoid(ga + ba) * attn_b + jax.nn.sigmoid(gs + bs) * ssm_b


def _qkv_kernel(q_ref, k_ref, v_ref, gq_ref, gk_ref, qn_ref, kf_ref, kb_ref, vf_ref, vb_ref, *, hd):
    gq = gq_ref[...] * (hd ** -0.5 * LOG2E)
    gk = gk_ref[...]
    for j in range(q_ref.shape[1] // hd):
        sl = slice(j * hd, (j + 1) * hd)
        q = q_ref[:, sl]
        qn = q * lax.rsqrt(jnp.mean(q * q, axis=-1, keepdims=True) + EPS) * gq
        qn_ref[:, sl] = qn.astype(BF16)
        k = k_ref[:, sl]
        kn = k * lax.rsqrt(jnp.mean(k * k, axis=-1, keepdims=True) + EPS) * gk
        kf_ref[:, sl] = kn
        kb_ref[:, sl] = kn.astype(BF16)
    v = v_ref[...]
    vf_ref[...] = v
    vb_ref[...] = v.astype(BF16)


def _qkv(proj, gq, gk, qk_width, v_width, hd):
    m = proj.shape[0]
    assert qk_width == v_width
    w = qk_width
    tm = _pick(m, (256, 128, 64, 32, 16, 8))
    spec = lambda c: pl.BlockSpec((tm, w), lambda i, c=c: (i, c))
    ospec = pl.BlockSpec((tm, w), lambda i: (i, 0))
    gspec = pl.BlockSpec((1, hd), lambda i: (0, 0))
    return pl.pallas_call(
        functools.partial(_qkv_kernel, hd=hd),
        grid=(m // tm,),
        in_specs=[spec(0), spec(1), spec(2), gspec, gspec],
        out_specs=[ospec] * 5,
        out_shape=[jax.ShapeDtypeStruct((m, w), dt) for dt in (BF16, F32, BF16, F32, BF16)],
        compiler_params=_cparams(("parallel",)),
    )(proj, proj, proj, gq.reshape(1, hd).astype(F32), gk.reshape(1, hd).astype(F32))


def _rel_bucket(rel, n_buckets):
    half = n_buckets // 2
    max_exact = half // 2
    n = jnp.abs(rel)
    nf = jnp.maximum(n, 1).astype(F32)
    large = max_exact + (jnp.log(nf / max_exact) / math.log(REL_MAX_DIST / max_exact)
                         * (half - max_exact)).astype(jnp.int32)
    large = jnp.minimum(large, half - 1)
    return jnp.where(rel > 0, half, 0) + jnp.where(n < max_exact, n, large)


def _bias_table(rel_bias, q_pos, k_pos):
    rel = k_pos[None, :] - q_pos[:, None]
    bucket = _rel_bucket(rel, rel_bias.shape[0])[None]
    table = rel_bias.astype(F32)
    bias = jnp.zeros((table.shape[1],) + rel.shape, F32)
    for b in range(table.shape[0]):
        bias = jnp.where(bucket == b, table[b][:, None, None], bias)
    visible = (k_pos[None, :] // CHUNK) <= (q_pos[:, None] // CHUNK)
    return jnp.where(visible[None], bias, NEG)


def _lambda(lam_ref, lam_init):
    lam = lam_ref[...]
    s1 = jnp.sum(lam[0:1] * lam[1:2], axis=-1, keepdims=True)
    s2 = jnp.sum(lam[2:3] * lam[3:4], axis=-1, keepdims=True)
    return jnp.exp(s1) - jnp.exp(s2) + lam_init


def _subln(o, g, lam_init):
    y = o * lax.rsqrt(jnp.mean(o * o, axis=-1, keepdims=True) + EPS)
    return (y * g) * (1.0 - lam_init)


_NT = (((1,), (1,)), ((), ()))


def _attn_prompt_kernel(qi_ref, kj_ref, ts_ref, q_ref, k_ref, v_ref, bias_ref, lam_ref, g_ref, o_ref,
                        m_sc, acc_sc, *, hd, lam_init):
    s = pl.program_id(2)
    qi = qi_ref[s]
    kj = kj_ref[s]

    @pl.when(kj == 0)
    def _():
        m_sc[...] = jnp.full_like(m_sc, NEG)
        acc_sc[...] = jnp.zeros_like(acc_sc)

    def step(with_bias):
        q = q_ref[...]
        k = k_ref[...]
        v = jnp.concatenate([v_ref[...], jnp.ones((k.shape[0], LANES), BF16)], axis=1)
        for c in range(2):
            sl = slice(c * hd, (c + 1) * hd)
            sc = lax.dot_general(q[:, sl], k[:, sl], _NT, preferred_element_type=F32)
            if with_bias:
                sc = sc + bias_ref[0, 0]
            m_prev = m_sc[c]
            m_new = jnp.maximum(m_prev, jnp.max(sc, axis=-1, keepdims=True))
            alpha = jnp.exp2(m_prev - m_new)
            p = jnp.exp2(sc - jnp.concatenate([m_new] * (sc.shape[1] // LANES), axis=1))
            acc_sc[c] = (jnp.concatenate([alpha] * (acc_sc.shape[2] // LANES), axis=1) * acc_sc[c]
                         + jnp.dot(p.astype(BF16), v, preferred_element_type=F32))
            m_sc[c] = m_new

    pl.when(ts_ref[s] < 2)(functools.partial(step, True))
    pl.when(ts_ref[s] == 2)(functools.partial(step, False))

    @pl.when(kj == qi)
    def _():
        lam = _lambda(lam_ref, lam_init)
        hw = 2 * hd
        o = []
        for c in range(2):
            acc = acc_sc[c]
            o.append(acc[:, :hw] / jnp.concatenate([acc[:, hw:]] * (hw // LANES), axis=1))
        o_ref[...] = _subln(o[0] - lam * o[1], g_ref[...], lam_init).astype(o_ref.dtype)


def _prompt_block(seq):
    blk = _pick(seq, (512, 256, 128))
    assert blk >= REL_MAX_DIST and blk % CHUNK == 0
    return blk


def _prompt_tables(rel_bias, seq):
    blk = _prompt_block(seq)
    pos = jnp.arange(blk, dtype=jnp.int32)
    far = _bias_table(rel_bias, pos[:1] + 2 * blk, pos[:1])
    near = jnp.stack([_bias_table(rel_bias, pos, pos), _bias_table(rel_bias, pos + blk, pos)])
    return jnp.where(near > 0.5 * NEG, (near - far[None]) * LOG2E, NEG)


def _attn_prompt(qn, kb, vb, bsz, seq, tables, lam4, g, n_heads, hd, lam_init):
    m, width = qn.shape
    hw = 2 * hd
    blk = _prompt_block(seq)
    nq = seq // blk
    pairs = [(i, j) for i in range(nq) for j in range(i + 1)]
    qi = jnp.array([p[0] for p in pairs], jnp.int32)
    kj = jnp.array([p[1] for p in pairs], jnp.int32)
    ts = jnp.array([min(p[0] - p[1], 2) for p in pairs], jnp.int32)
    grid_spec = pltpu.PrefetchScalarGridSpec(
        num_scalar_prefetch=3,
        grid=(bsz, n_heads, len(pairs)),
        in_specs=[
            pl.BlockSpec((blk, hw), lambda b, h, s, qi, kj, ts: (b * nq + qi[s], h)),
            pl.BlockSpec((blk, hw), lambda b, h, s, qi, kj, ts: (b * nq + kj[s], h)),
            pl.BlockSpec((blk, hw), lambda b, h, s, qi, kj, ts: (b * nq + kj[s], h)),
            pl.BlockSpec((1, 1, blk, blk), lambda b, h, s, qi, kj, ts: (jnp.minimum(ts[s], 1), h, 0, 0)),
            pl.BlockSpec((4, hd), lambda b, h, s, qi, kj, ts: (0, 0)),
            pl.BlockSpec((1, hw), lambda b, h, s, qi, kj, ts: (0, 0)),
        ],
        out_specs=pl.BlockSpec((blk, hw), lambda b, h, s, qi, kj, ts: (b * nq + qi[s], h)),
        scratch_shapes=[pltpu.VMEM((2, blk, LANES), F32),
                        pltpu.VMEM((2, blk, hw + LANES), F32)],
    )
    return pl.pallas_call(
        functools.partial(_attn_prompt_kernel, hd=hd, lam_init=lam_init),
        grid_spec=grid_spec,
        out_shape=jax.ShapeDtypeStruct((m, width), BF16),
        compiler_params=_cparams(("parallel", "parallel", "arbitrary")),
    )(qi, kj, ts, qn, kb, vb, tables, lam4, g)


def _attn_sample_kernel(q_ref, ck_ref, cv_ref, nk_ref, nv_ref, bp_ref, bn_ref, lam_ref, g_ref, o_in, o_ref,
                        *, hd, n_heads, past, lam_init):
    del o_in
    lam = _lambda(lam_ref, lam_init)
    hw = 2 * hd
    for h in range(n_heads):
        cols = slice(h * hw, (h + 1) * hw)
        q = q_ref[:, cols]
        nk = nk_ref[:, cols]
        nv = nv_ref[:, cols]
        cv = cv_ref[0, 0, :, cols].astype(BF16)
        bp = bp_ref[h]
        bn = bn_ref[h]
        probs = []
        for c in range(2):
            sl = slice(c * hd, (c + 1) * hd)
            ck = ck_ref[0, 0, pl.ds(2 * h + c, past, stride=2 * n_heads), :].astype(BF16)
            sp = lax.dot_general(q[:, sl], ck, _NT, preferred_element_type=F32) + bp
            sn = lax.dot_general(q[:, sl], nk[:, sl], _NT, preferred_element_type=F32) + bn
            m = jnp.maximum(jnp.max(sp, axis=-1, keepdims=True), jnp.max(sn, axis=-1, keepdims=True))
            pp = jnp.exp2(sp - m)
            pn = jnp.exp2(sn - m)
            inv = 1.0 / (jnp.sum(pp, axis=-1, keepdims=True) + jnp.sum(pn, axis=-1, keepdims=True))
            probs.append((pp * inv, pn * inv))
        wp = probs[0][0] - lam * probs[1][0]
        wn = probs[0][1] - lam * probs[1][1]
        o = (jnp.dot(wp.astype(BF16), cv, preferred_element_type=F32)
             + jnp.dot(wn.astype(BF16), nv, preferred_element_type=F32))
        o_ref[:, cols] = _subln(o, g_ref[...], lam_init).astype(o_ref.dtype)


def _sample_table(rel_bias, past, s_len):
    table = _bias_table(rel_bias, past + jnp.arange(s_len, dtype=jnp.int32),
                        jnp.arange(past + s_len, dtype=jnp.int32))
    return jnp.where(table > 0.5 * NEG, table * LOG2E, NEG)


def _attn_sample(qn, kb, vb, o, row0, bsz, s_len, cache_k, cache_v, layer, table, lam4, g, n_heads, hd, lam_init):
    width = qn.shape[1]
    hw = 2 * hd
    past = cache_v.shape[2]
    assert row0 % s_len == 0
    new_spec = pl.BlockSpec((s_len, width), lambda b: (row0 // s_len + b, 0))
    return pl.pallas_call(
        functools.partial(_attn_sample_kernel, hd=hd, n_heads=n_heads, past=past, lam_init=lam_init),
        grid=(bsz,),
        in_specs=[new_spec,
                  pl.BlockSpec((1, 1, past * n_heads * 2, hd), lambda b: (layer, b, 0, 0)),
                  pl.BlockSpec((1, 1, past, width), lambda b: (layer, b, 0, 0)),
                  new_spec, new_spec,
                  pl.BlockSpec((n_heads, s_len, past), lambda b: (0, 0, 0)),
                  pl.BlockSpec((n_heads, s_len, s_len), lambda b: (0, 0, 0)),
                  pl.BlockSpec((4, hd), lambda b: (0, 0)),
                  pl.BlockSpec((1, hw), lambda b: (0, 0)),
                  pl.BlockSpec(memory_space=pl.ANY)],
        out_specs=new_spec,
        out_shape=jax.ShapeDtypeStruct(o.shape, o.dtype),
        input_output_aliases={9: 0},
        compiler_params=_cparams(("parallel",)),
    )(qn, cache_k, cache_v, kb, vb, table[:, :, :past], table[:, :, past:], lam4, g, o)


def _split(x):
    hi = x.astype(BF16)
    lo = (x - hi.astype(F32)).astype(BF16)
    return hi, lo


def _dot3(a, b):
    ah, al = _split(a)
    bh, bl = _split(b)
    d = functools.partial(jnp.dot, preferred_element_type=F32)
    return d(ah, bh) + (d(ah, bl) + d(al, bh))


def _s5_operators(a_re, a_im, log_dt, b_re, b_im, c_re, c_im, d):
    hp = lax.Precision.HIGHEST
    g, n = a_re.shape
    c = b_re.shape[-1]
    t = S5_T
    assert 2 * n == LANES and LANES % c == 0
    dt = jnp.exp(log_dt.astype(F32))[:, None]
    lr = a_re.astype(F32)
    li = a_im.astype(F32)
    mag = jnp.exp(lr * dt)
    ab_re = mag * jnp.cos(li * dt)
    ab_im = mag * jnp.sin(li * dt)
    den = lr * lr + li * li
    num_re = ab_re - 1.0
    coef_re = (num_re * lr + ab_im * li) / den
    coef_im = (ab_im * lr - num_re * li) / den
    br = b_re.astype(F32)
    bi = b_im.astype(F32)
    bb_re = coef_re[..., None] * br - coef_im[..., None] * bi
    bb_im = coef_re[..., None] * bi + coef_im[..., None] * br
    pr = [jnp.ones_like(ab_re)]
    pi = [jnp.zeros_like(ab_re)]
    for _ in range(t):
        pr.append(pr[-1] * ab_re - pi[-1] * ab_im)
        pi.append(pr[-2] * ab_im + pi[-1] * ab_re)
    p_re = jnp.stack(pr)
    p_im = jnp.stack(pi)
    cr = c_re.astype(F32)
    ci = c_im.astype(F32)
    cp_re = cr[None] * p_re[:, :, None, :] - ci[None] * p_im[:, :, None, :]
    cp_im = cr[None] * p_im[:, :, None, :] + ci[None] * p_re[:, :, None, :]
    kk = (jnp.einsum("jgon,gni->gjoi", cp_re[:t], bb_re, precision=hp)
          - jnp.einsum("jgon,gni->gjoi", cp_im[:t], bb_im, precision=hp))
    oc = LANES // c
    go = g // oc
    w = jnp.einsum("ogjci,gh->ojgihc", kk.reshape(go, oc, t, c, c), jnp.eye(oc, dtype=F32))
    w = w.reshape(go, t, LANES, LANES).astype(BF16)
    pw_re = p_re[:t][::-1]
    pw_im = p_im[:t][::-1]
    min_re = pw_re[:, :, :, None] * bb_re[None] - pw_im[:, :, :, None] * bb_im[None]
    min_im = pw_re[:, :, :, None] * bb_im[None] + pw_im[:, :, :, None] * bb_re[None]
    m_in = jnp.concatenate([jnp.transpose(min_re, (1, 0, 3, 2)), jnp.transpose(min_im, (1, 0, 3, 2))], axis=-1)
    m_in = jnp.transpose(m_in.reshape(go, oc, t, c, 2 * n), (0, 2, 1, 3, 4)).reshape(go, t * LANES, 2 * n)
    m_out = jnp.concatenate([cp_re[1:], -cp_im[1:]], axis=-1)
    m_out = jnp.transpose(m_out.reshape(t, go, oc, c, 2 * n), (1, 0, 2, 3, 4)).reshape(go, t * LANES, 2 * n)
    a_t_re, a_t_im = p_re[t], p_im[t]
    a1 = jnp.concatenate([a_t_re, a_t_re], axis=-1).reshape(1, g * 2 * n)
    a2 = jnp.concatenate([-a_t_im, a_t_im], axis=-1).reshape(1, g * 2 * n)
    return dict(w=w, m_in=m_in, m_out=m_out, a1=a1, a2=a2, d_row=d.astype(F32).reshape(1, g * c))


def _toeplitz_kernel(w_ref, o_ref, *, t):
    zero = jnp.zeros((LANES, LANES), o_ref.dtype)
    for s in range(t):
        for u in range(t):
            o_ref[0, s * LANES:(s + 1) * LANES, u * LANES:(u + 1) * LANES] = w_ref[0, u - s] if u >= s else zero


def _toeplitz(w):
    go, t = w.shape[:2]
    return pl.pallas_call(
        functools.partial(_toeplitz_kernel, t=t),
        grid=(go,),
        in_specs=[pl.BlockSpec((1, t, LANES, LANES), lambda o: (o, 0, 0, 0))],
        out_specs=pl.BlockSpec((1, t * LANES, t * LANES), lambda o: (o, 0, 0)),
        out_shape=jax.ShapeDtypeStruct((go, t * LANES, t * LANES), w.dtype),
        compiler_params=_cparams(("parallel",)),
    )(w)


def _expand(compact, c):
    oc = LANES // c
    tiled = jnp.concatenate([compact] * oc, axis=1)
    r = lax.broadcasted_iota(jnp.int32, tiled.shape, 0)
    q = lax.broadcasted_iota(jnp.int32, tiled.shape, 1)
    own = ((r & (LANES - 1)) >> int(math.log2(c))) == (q >> int(math.log2(LANES)))
    return jnp.where(own, tiled, 0.0)


def _gather_chunks(u_ref, xcat, t, rows):
    for s in range(t):
        xcat[:, s * LANES:(s + 1) * LANES] = u_ref[pl.ds(s, rows, stride=t), :]


def _s5_state_kernel(u_ref, min_ref, v_ref, xcat, *, t, rows, c):
    _gather_chunks(u_ref, xcat, t, rows)
    v_ref[...] = _dot3(xcat[...], _expand(min_ref[0], c))


def _s5_scan_kernel(v_ref, a1_ref, a2_ref, h0_ref, h_ref, f_ref, vs_sc, *, nchunk, half):
    a1 = a1_ref[...]
    a2 = a2_ref[...]
    lt = a1.shape[1]

    def swap(x):
        lane = lax.broadcasted_iota(jnp.int32, x.shape, 1)
        is_re = (lane & (2 * half - 1)) < half
        return jnp.where(is_re, pltpu.roll(x, lt - half, axis=1), pltpu.roll(x, half, axis=1))

    vs_sc[...] = swap(v_ref[...])
    a2s = swap(a2)
    h0 = h0_ref[...]

    def body(k, carry):
        h, g = carry
        h_ref[pl.ds(k, 1), :] = h
        return (a1 * h + a2 * g + v_ref[pl.ds(k, 1), :], a1 * g + a2s * h + vs_sc[pl.ds(k, 1), :])

    h, _ = lax.fori_loop(0, nchunk, body, (h0, swap(h0)))
    f_ref[...] = h


def _s5_out_kernel(u_ref, mi_ref, mo_ref, h_ref, d_ref, o_in, o_ref, xcat, *, t, rows, c):
    del o_in
    _gather_chunks(u_ref, xcat, t, rows)
    mo = _expand(mo_ref[0], c).astype(BF16)
    y = (jnp.dot(xcat[...].astype(BF16), mi_ref[0], preferred_element_type=F32)
         + lax.dot_general(h_ref[...].astype(BF16), mo, _NT, preferred_element_type=F32))
    d = d_ref[...]
    for s in range(t):
        sl = slice(s * LANES, (s + 1) * LANES)
        o_ref[pl.ds(s, rows, stride=t), :] = jax.nn.gelu(y[:, sl] + xcat[:, sl] * d)


def _s5(proj, hs, row0, bsz, seq, ucol, h0, ops, mi, groups):
    t = S5_T
    width = ops["d_row"].shape[1]
    c = width // groups
    gl = ops["a1"].shape[1]
    noct = width // LANES
    ocl = gl // noct
    nchunk = seq // t
    tokens = bsz * seq
    tb = _pick(tokens, (8192, 4096, 2048, 1024, 512, 256, 128))
    rb = tb // t
    assert row0 % tb == 0 and ucol % LANES == 0 and seq % t == 0
    u_spec = pl.BlockSpec((tb, LANES), lambda o, r: (row0 // tb + r, ucol // LANES + o))
    st_spec = pl.BlockSpec((rb, ocl), lambda o, r: (r, o))
    cmp_spec = pl.BlockSpec((1, t * LANES, LANES), lambda o, r: (o, 0, 0))
    v = pl.pallas_call(
        functools.partial(_s5_state_kernel, t=t, rows=rb, c=c),
        grid=(noct, tokens // tb),
        in_specs=[u_spec, cmp_spec],
        out_specs=st_spec,
        out_shape=jax.ShapeDtypeStruct((tokens // t, gl), F32),
        scratch_shapes=[pltpu.VMEM((rb, t * LANES), F32)],
        compiler_params=_cparams(("parallel", "parallel")),
    )(proj, ops["m_in"])

    lt = _pick(gl, (2048, 1024, 512, 256, 128))
    seq_blk = pl.BlockSpec((None, nchunk, lt), lambda b, j: (b, 0, j))
    vec_blk = pl.BlockSpec((1, lt), lambda b, j: (0, j))
    st_blk = pl.BlockSpec((None, 1, lt), lambda b, j: (b, 0, j))
    h, f = pl.pallas_call(
        functools.partial(_s5_scan_kernel, nchunk=nchunk, half=LANES // 2),
        grid=(bsz, gl // lt),
        in_specs=[seq_blk, vec_blk, vec_blk, st_blk],
        out_specs=[seq_blk, st_blk],
        out_shape=[jax.ShapeDtypeStruct((bsz, nchunk, gl), F32), jax.ShapeDtypeStruct((bsz, 1, gl), F32)],
        scratch_shapes=[pltpu.VMEM((nchunk, lt), F32)],
        compiler_params=_cparams(("parallel", "parallel")),
    )(v.reshape(bsz, nchunk, gl), ops["a1"], ops["a2"], h0.reshape(bsz, 1, gl))

    hs = pl.pallas_call(
        functools.partial(_s5_out_kernel, t=t, rows=rb, c=c),
        grid=(noct, tokens // tb),
        in_specs=[u_spec,
                  pl.BlockSpec((1, t * LANES, t * LANES), lambda o, r: (o, 0, 0)),
                  cmp_spec, st_spec,
                  pl.BlockSpec((1, LANES), lambda o, r: (0, o)),
                  pl.BlockSpec(memory_space=pl.ANY)],
        out_specs=pl.BlockSpec((tb, LANES), lambda o, r: (row0 // tb + r, o)),
        out_shape=jax.ShapeDtypeStruct(hs.shape, hs.dtype),
        input_output_aliases={5: 0},
        scratch_shapes=[pltpu.VMEM((rb, t * LANES), F32)],
        compiler_params=_cparams(("parallel", "parallel")),
    )(proj, mi, ops["m_out"], h.reshape(tokens // t, gl), ops["d_row"], hs)
    return hs, f.reshape(bsz, gl)


def _interleave(re, im):
    return jnp.concatenate([re.astype(F32), im.astype(F32)], axis=-1).reshape(re.shape[0], -1)


def _deinterleave(h, groups):
    h = h.reshape(h.shape[0], groups, 2, -1)
    return h[:, :, 0], h[:, :, 1]


def _router_kernel(x_ref, w_ref, b_ref, o_ref, cnt_ref, carry):
    i = pl.program_id(0)

    @pl.when(i == 0)
    def _():
        carry[...] = jnp.zeros_like(carry)

    tm = x_ref.shape[0]
    logits = jnp.dot(x_ref[...].astype(BF16), w_ref[...], preferred_element_type=F32) + b_ref[...]
    lane = lax.broadcasted_iota(jnp.int32, logits.shape, 1)
    m1 = jnp.max(logits, axis=-1, keepdims=True)
    i1 = jnp.min(jnp.where(logits == m1, lane, LANES), axis=-1, keepdims=True)
    rest = jnp.where(lane == i1, -3e38, logits)
    m2 = jnp.max(rest, axis=-1, keepdims=True)
    i2 = jnp.min(jnp.where(rest == m2, lane, LANES), axis=-1, keepdims=True)
    e = jnp.exp(m2 - m1)
    w1 = 1.0 / (1.0 + e)
    w2 = e / (1.0 + e)
    onehot = jnp.where((lane == i1) | (lane == i2), 1.0, 0.0)
    r = lax.broadcasted_iota(jnp.int32, (tm, tm), 0)
    c = lax.broadcasted_iota(jnp.int32, (tm, tm), 1)
    tri = jnp.where(c < r, 1.0, 0.0).astype(BF16)
    rank = jnp.dot(tri, onehot.astype(BF16), preferred_element_type=F32) + carry[...]
    r1 = jnp.sum(jnp.where(lane == i1, rank, 0.0), axis=-1, keepdims=True)
    r2 = jnp.sum(jnp.where(lane == i2, rank, 0.0), axis=-1, keepdims=True)
    carry[...] += jnp.sum(onehot, axis=0, keepdims=True)
    cols = (i1.astype(F32), i2.astype(F32), w1, w2, r1, r2)
    out = jnp.zeros(logits.shape, F32)
    for idx, col in enumerate(cols):
        out = jnp.where(lane == idx, col, out)
    o_ref[...] = out
    cnt_ref[...] = carry[...]


def _router(xn, w_router, b_router):
    m, d = xn.shape
    e = w_router.shape[1]
    assert e <= LANES
    tm = _pick(m, (512, 256, 128, 64, 32, 16, 8))
    w = jnp.zeros((d, LANES), BF16).at[:, :e].set(w_router.astype(BF16))
    b = jnp.full((1, LANES), NEG, F32).at[0, :e].set(b_router.astype(F32))
    return pl.pallas_call(
        _router_kernel,
        grid=(m // tm,),
        in_specs=[pl.BlockSpec((tm, d), lambda i: (i, 0)), pl.BlockSpec((d, LANES), lambda i: (0, 0)),
                  pl.BlockSpec((1, LANES), lambda i: (0, 0))],
        out_specs=[pl.BlockSpec((tm, LANES), lambda i: (i, 0)), pl.BlockSpec((1, LANES), lambda i: (0, 0))],
        out_shape=[jax.ShapeDtypeStruct((m, LANES), F32), jax.ShapeDtypeStruct((1, LANES), F32)],
        scratch_shapes=[pltpu.VMEM((1, LANES), F32)],
        compiler_params=_cparams(("arbitrary",)),
    )(xn, w, b)


def _row_copy(src, dst, src_row, dst_row, sem):
    return pltpu.make_async_copy(src.at[pl.ds(src_row, 1)], dst.at[pl.ds(dst_row, 1)], sem)


def _dispatch_kernel(pos_ref, x_ref, zeros_hbm, xs_hbm, sem, *, tb):
    del zeros_hbm
    base = pl.program_id(0) * tb

    def issue(t, carry):
        for s in range(2):
            _row_copy(x_ref, xs_hbm, t, pos_ref[2 * (base + t) + s], sem).start(priority=s)
        return carry

    lax.fori_loop(0, tb, issue, 0)

    def drain(t, carry):
        for s in range(2):
            _row_copy(x_ref, xs_hbm, 0, 0, sem).wait()
        return carry

    lax.fori_loop(0, tb, drain, 0)


def _dispatch(xn, pos, rows_padded):
    m, d = xn.shape
    tb = _pick(m, (512, 256, 128, 64, 32, 16, 8))
    grid_spec = pltpu.PrefetchScalarGridSpec(
        num_scalar_prefetch=1,
        grid=(m // tb,),
        in_specs=[pl.BlockSpec((tb, d), lambda i, pos: (i, 0)), pl.BlockSpec(memory_space=pl.ANY)],
        out_specs=pl.BlockSpec(memory_space=pl.ANY),
        scratch_shapes=[pltpu.SemaphoreType.DMA(())],
    )
    return pl.pallas_call(
        functools.partial(_dispatch_kernel, tb=tb),
        grid_spec=grid_spec,
        out_shape=jax.ShapeDtypeStruct((rows_padded, d), xn.dtype),
        input_output_aliases={2: 0},
        compiler_params=pltpu.CompilerParams(dimension_semantics=("arbitrary",), has_side_effects=True,
                                             vmem_limit_bytes=VMEM_LIMIT),
    )(pos, xn, jnp.zeros((rows_padded, d), xn.dtype))


def _combine_kernel(pos_ref, ys_hbm, x_ref, r_ref, o_ref, buf, sem, *, tb):
    base = pl.program_id(0) * tb

    def issue(t, carry):
        for s in range(2):
            _row_copy(ys_hbm, buf.at[s], pos_ref[2 * (base + t) + s], t, sem).start(priority=s)
        return carry

    lax.fori_loop(0, tb, issue, 0)

    def drain(t, carry):
        for s in range(2):
            _row_copy(ys_hbm, buf.at[s], 0, 0, sem).wait()
        return carry

    lax.fori_loop(0, tb, drain, 0)
    r = r_ref[...]
    o_ref[...] = x_ref[...] + (r[:, 2:3] * buf[0] + r[:, 3:4] * buf[1])


def _combine(ys, pos, x, routing):
    m, d = x.shape
    tb = _pick(m, (256, 128, 64, 32, 16, 8))
    grid_spec = pltpu.PrefetchScalarGridSpec(
        num_scalar_prefetch=1,
        grid=(m // tb,),
        in_specs=[pl.BlockSpec(memory_space=pl.ANY),
                  pl.BlockSpec((tb, d), lambda i, pos: (i, 0)),
                  pl.BlockSpec((tb, LANES), lambda i, pos: (i, 0))],
        out_specs=pl.BlockSpec((tb, d), lambda i, pos: (i, 0)),
        scratch_shapes=[pltpu.VMEM((2, tb, d), F32), pltpu.SemaphoreType.DMA(())],
    )
    return pl.pallas_call(
        functools.partial(_combine_kernel, tb=tb),
        grid_spec=grid_spec,
        out_shape=jax.ShapeDtypeStruct((m, d), F32),
        compiler_params=_cparams(("arbitrary",)),
    )(pos, ys, x, routing)


def _moe(x, xn, w_router, b_router, wg, wu, wd):
    m, d = x.shape
    n_exp = wg.shape[0]
    tg = 512 if m >= 4096 else 32
    n_tiles = -(-(2 * m + n_exp * (tg - 1)) // tg)
    rows_padded = n_tiles * tg
    routing, counts = _router(xn, w_router, b_router)
    cnt = counts[0, :n_exp].astype(jnp.int32)
    padded = ((cnt + tg - 1) // tg) * tg
    ends = jnp.cumsum(padded)
    offs = ends - padded
    lanes = jnp.arange(LANES, dtype=jnp.int32)
    offs_row = jnp.zeros((LANES,), jnp.int32).at[:n_exp].set(offs)

    def row_of(col_expert, col_rank):
        e = routing[:, col_expert].astype(jnp.int32)
        off = jnp.sum(jnp.where(e[:, None] == lanes[None, :], offs_row[None, :], 0), axis=1)
        return off + routing[:, col_rank].astype(jnp.int32)

    pos = jnp.stack([row_of(0, 4), row_of(1, 5)], axis=1).reshape(-1)
    n_active = (ends[-1] // tg).reshape(1).astype(jnp.int32)
    tile_start = jnp.arange(n_tiles, dtype=jnp.int32) * tg
    tile_expert = jnp.minimum(jnp.sum(tile_start[:, None] >= ends[None, :], axis=1), n_exp - 1).astype(jnp.int32)
    xs = _dispatch(xn, pos, rows_padded)
    group = (tile_expert, n_active)
    f = wg.shape[-1]
    h = _matmul(xs, [wg, wu], _epi_swiglu, BF16, group=group, tm=tg,
                tn=_pick(f, (1408, 1024, 512, 256, 128)), tk=_pick(d, (2048, 1024, 512, 256, 128)))
    ys = _matmul(h, [wd], _epi_id, F32, group=group, tm=tg, tk=f)
    return _combine(ys, pos, x, routing)


def kernel(x_prompt, x_sample, cache_k, cache_v, state_ssm_re, state_ssm_im, rel_bias, norm_mix_g, w_in, b_gate, q_norm_g, k_norm_g, lambda_q1, lambda_k1, lambda_q2, lambda_k2, subln_g, w_attn_proj, ssm_a_re, ssm_a_im, ssm_log_dt, ssm_b_re, ssm_b_im, ssm_c_re, ssm_c_im, ssm_d, w_glu, b_glu, w_ssm_proj, w_out, norm_ffn_g, w_ff_gate, w_ff_up, w_ff_down, w_router, b_router, w_e_gate, w_e_up, w_e_down):
    bp, lp, d = x_prompt.shape
    bs, ls, _ = x_sample.shape
    depth = w_in.shape[0]
    past = cache_k.shape[2]
    n_heads, hd = cache_k.shape[3], cache_k.shape[5]
    qkw = n_heads * 2 * hd
    groups, n_state = state_ssm_re.shape[2], state_ssm_re.shape[3]
    ssm_w = ssm_d.shape[1]
    mp, ms = bp * lp, bs * ls
    assert hd % LANES == 0 and lp % S5_T == 0 and ls % S5_T == 0 and ssm_w % LANES == 0

    ck = cache_k.reshape(depth, bs, past * n_heads * 2, hd)
    cv = cache_v.reshape(depth, bs, past, qkw)
    x = jnp.concatenate([x_prompt.reshape(mp, d), x_sample.reshape(ms, d)], axis=0)
    m = mp + ms

    d_ff = w_ff_gate.shape[-1]
    ff_pad = (-d_ff) % 1024 if d_ff > 1024 else 0
    outs = {k: [] for k in ("kp", "vp", "hrp", "hip", "ks", "vs", "hrs", "his")}
    zeros_state = jnp.zeros((bp, groups * 2 * n_state), F32)
    tables_p = _prompt_tables(rel_bias, lp)
    table_s = _sample_table(rel_bias, past, ls)

    for l in range(depth):
        lam_init = 0.8 - 0.6 * math.exp(-0.3 * l)
        xn = _rmsnorm(x, norm_mix_g[l], BF16)
        proj = _matmul(xn, [w_in[l].astype(BF16)], _epi_id, F32)
        qn, kf, kb, vf, vb = _qkv(proj, q_norm_g[l], k_norm_g[l], qkw, qkw, hd)
        lam4 = jnp.stack([lambda_q1[l], lambda_k1[l], lambda_q2[l], lambda_k2[l]]).astype(F32)
        g_sub = subln_g[l].reshape(1, 2 * hd).astype(F32)
        o = _attn_prompt(qn, kb, vb, bp, lp, tables_p, lam4, g_sub, n_heads, hd, lam_init)
        o = _attn_sample(qn, kb, vb, o, mp, bs, ls, ck, cv, l, table_s, lam4, g_sub, n_heads, hd, lam_init)
        attn_b = _matmul(o, [w_attn_proj[l].astype(BF16)], _epi_id, F32)

        ops = _s5_operators(ssm_a_re[l], ssm_a_im[l], ssm_log_dt[l], ssm_b_re[l], ssm_b_im[l],
                            ssm_c_re[l], ssm_c_im[l], ssm_d[l])
        mi = _toeplitz(ops["w"])
        hs = jnp.zeros((m, ssm_w), F32)
        hs, h_p = _s5(proj, hs, 0, bp, lp, 3 * qkw, zeros_state, ops, mi, groups)
        hs, h_s = _s5(proj, hs, mp, bs, ls, 3 * qkw, _interleave(state_ssm_re[l], state_ssm_im[l]), ops, mi, groups)
        hr_p, hi_p = _deinterleave(h_p, groups)
        hr_s, hi_s = _deinterleave(h_s, groups)
        hs = _matmul(hs, [w_glu[l].astype(BF16)], _epi_glu, BF16,
                     extras=[(hs, "tile", 0), (b_glu[l].reshape(1, -1).astype(F32), "row", 0)])
        tn = _pick(math.gcd(d, 3 * qkw + ssm_w), (1024, 512, 256, 128))
        goff = (3 * qkw + ssm_w) // tn
        bg = b_gate[l].reshape(1, -1).astype(F32)
        mix = _matmul(hs, [w_ssm_proj[l].astype(BF16)], _epi_mix, BF16, tn=tn,
                      extras=[(attn_b, "tile", 0), (proj, "tile", goff), (proj, "tile", goff + d // tn),
                              (bg, "row", 0), (bg, "row", d // tn)])
        x = _matmul(mix, [w_out[l].astype(BF16)], _epi_residual, F32, extras=[(x, "tile", 0)])

        i = l // 2
        if l % 2 == 0:
            xn = _rmsnorm(x, norm_ffn_g[l], BF16)
            wg = jnp.pad(w_ff_gate[i].astype(BF16), ((0, 0), (0, ff_pad)))
            wu = jnp.pad(w_ff_up[i].astype(BF16), ((0, 0), (0, ff_pad)))
            wd = jnp.pad(w_ff_down[i].astype(BF16), ((0, ff_pad), (0, 0)))
            h = _matmul(xn, [wg, wu], _epi_swiglu, BF16)
            x = _matmul(h, [wd], _epi_residual, F32, extras=[(x, "tile", 0)])
        else:
            xn = _rmsnorm(x, norm_ffn_g[l], F32)
            x = _moe(x, xn, w_router[i], b_router[i], w_e_gate[i].astype(BF16), w_e_up[i].astype(BF16),
                     w_e_down[i].astype(BF16))

        outs["kp"].append(kf[:mp].reshape(bp, lp, n_heads, 2, hd))
        outs["vp"].append(vf[:mp].reshape(bp, lp, n_heads, 2 * hd))
        outs["hrp"].append(hr_p)
        outs["hip"].append(hi_p)
        outs["ks"].append(kf[mp:].reshape(bs, ls, n_heads, 2, hd))
        outs["vs"].append(vf[mp:].reshape(bs, ls, n_heads, 2 * hd))
        outs["hrs"].append(hr_s)
        outs["his"].append(hi_s)

    st = {k: jnp.stack(v) for k, v in outs.items()}
    return (x[:mp].reshape(bp, lp, d), x[mp:].reshape(bs, ls, d), st["kp"], st["vp"], st["hrp"], st["hip"],
            st["ks"], st["vs"], st["hrs"], st["his"])
```
